```python
import jax, jax.numpy as jnp
from jax import lax
import numpy as np

D_MODEL = 2048
BATCH = 1
SEQ = 16384
DEPTH = 2

HEAD_DIM = 128
ATTN_WIDTH = D_MODEL // 2
N_ATTN_HEADS = ATTN_WIDTH // HEAD_DIM
CONV_CH = D_MODEL // 4
CONV_GROUPS = CONV_CH // HEAD_DIM
CONV_K = 3
MEM_WIDTH = D_MODEL // 4
N_MEM_HEADS = MEM_WIDTH // HEAD_DIM
MEM_TOKENS = 256
MIX_WIDTH = ATTN_WIDTH + CONV_CH + MEM_WIDTH
IN_WIDTH = 3 * ATTN_WIDTH + 3 * CONV_CH + MEM_WIDTH
IN_SPLITS = [ATTN_WIDTH, 2 * ATTN_WIDTH, 3 * ATTN_WIDTH,
             3 * ATTN_WIDTH + CONV_CH, 3 * ATTN_WIDTH + 2 * CONV_CH,
             3 * ATTN_WIDTH + 3 * CONV_CH]
D_FF = 11 * D_MODEL // 4
FFN_RES = 0.5
MOBA_BLOCK = 256
MOBA_TOPK = 3
Q_CHUNK = 128
ROPE_THETA = 10000.0
RMS_EPS = 1e-6

kernel_name = "hymba_moba_shortconv_memory_macaron"


def rms_norm(x, g):
    xf = x.astype(jnp.float32)
    y = xf * lax.rsqrt(jnp.mean(xf * xf, axis=-1, keepdims=True) + RMS_EPS)
    return (y * g.astype(jnp.float32)).astype(x.dtype)


def swiglu(h, w_gate_up, w_down):
    g, u = jnp.split(h @ w_gate_up, 2, axis=-1)
    return (jax.nn.silu(g) * u) @ w_down


def rope_tables(positions):
    inv_freq = ROPE_THETA ** (-jnp.arange(0, HEAD_DIM, 2, dtype=jnp.float32) / HEAD_DIM)
    ang = positions.astype(jnp.float32)[..., None] * inv_freq
    return jnp.cos(ang)[:, :, None, :], jnp.sin(ang)[:, :, None, :]


def apply_rope(x, cos, sin):
    xf = x.astype(jnp.float32)
    x1, x2 = jnp.split(xf, 2, axis=-1)
    return jnp.concatenate([x1 * cos - x2 * sin, x2 * cos + x1 * sin], axis=-1).astype(x.dtype)


def moba_attention(q, k, v):
    b, s, h, d = q.shape
    n_blk = -(-s // MOBA_BLOCK)
    s_pad = n_blk * MOBA_BLOCK
    top_k = min(MOBA_TOPK, n_blk)
    scale = d ** -0.5
    qh = q.transpose(0, 2, 1, 3)
    pad = ((0, 0), (0, 0), (0, s_pad - s), (0, 0))
    kh = jnp.pad(k.transpose(0, 2, 1, 3), pad)
    vh = jnp.pad(v.transpose(0, 2, 1, 3), pad)
    k_blocks = kh.reshape(b, h, n_blk, MOBA_BLOCK, d)
    v_blocks = vh.reshape(b, h, n_blk, MOBA_BLOCK, d)
    k_mean = jnp.mean(k_blocks.astype(jnp.float32), axis=3)
    bi = jnp.arange(b)[:, None, None, None]
    hi = jnp.arange(h)[None, :, None, None]
    blk_ids = jnp.arange(n_blk)

    def chunk(ci):
        start = ci * Q_CHUNK
        qblk = start // MOBA_BLOCK
        qc = lax.dynamic_slice_in_dim(qh, start, Q_CHUNK, axis=2)
        gate = jnp.einsum('bhqd,bhnd->bhqn', qc.astype(jnp.float32), k_mean)
        gate = jnp.where(blk_ids < qblk, gate, -jnp.inf)
        _, idx = lax.top_k(gate, top_k)
        valid = idx < qblk
        k_sel = k_blocks[bi, hi, idx]
        v_sel = v_blocks[bi, hi, idx]
        s_sel = jnp.einsum('bhqd,bhqtkd->bhqtk', qc, k_sel,
                           preferred_element_type=jnp.float32) * scale
        s_sel = jnp.where(valid[..., None], s_sel, -jnp.inf)
        blk_start = qblk * MOBA_BLOCK
        k_own = lax.dynamic_slice_in_dim(kh, blk_start, MOBA_BLOCK, axis=2)
        v_own = lax.dynamic_slice_in_dim(vh, blk_start, MOBA_BLOCK, axis=2)
        s_own = jnp.einsum('bhqd,bhkd->bhqk', qc, k_own,
                           preferred_element_type=jnp.float32) * scale
        q_pos = start + jnp.arange(Q_CHUNK)
        k_pos = blk_start + jnp.arange(MOBA_BLOCK)
        s_own = jnp.where(k_pos[None, :] <= q_pos[:, None], s_own, -jnp.inf)
        n_sel = top_k * MOBA_BLOCK
        logits = jnp.concatenate([s_sel.reshape(b, h, Q_CHUNK, n_sel), s_own], axis=-1)
        p = jax.nn.softmax(logits, axis=-1).astype(v.dtype)
        p_sel = p[..., :n_sel].reshape(b, h, Q_CHUNK, top_k, MOBA_BLOCK)
        p_own = p[..., n_sel:]
        return (jnp.einsum('bhqtk,bhqtkd->bhqd', p_sel, v_sel)
                + jnp.einsum('bhqk,bhkd->bhqd', p_own, v_own))

    out = lax.map(chunk, jnp.arange(s // Q_CHUNK))
    return out.transpose(1, 0, 3, 2, 4).reshape(b, s, h * d)


def short_conv(u, w):
    s = u.shape[1]
    up = jnp.pad(u, ((0, 0), (CONV_K - 1, 0), (0, 0)))
    y = up[:, 0:s] * w[0]
    for j in range(1, CONV_K):
        y = y + up[:, j:j + s] * w[j]
    return y


def memory_attention(mq, mkv, q_g, k_g):
    b, s, _ = mq.shape
    m = mkv.shape[1]
    q = rms_norm(mq.reshape(b, s, N_MEM_HEADS, HEAD_DIM), q_g)
    mk, mv = jnp.split(mkv, 2, axis=-1)
    k = rms_norm(mk.reshape(b, m, N_MEM_HEADS, HEAD_DIM), k_g)
    v = mv.reshape(b, m, N_MEM_HEADS, HEAD_DIM)
    sc = jnp.einsum('bshd,bmhd->bhsm', q, k, preferred_element_type=jnp.float32) * HEAD_DIM ** -0.5
    p = jax.nn.softmax(sc, axis=-1).astype(v.dtype)
    return jnp.einsum('bhsm,bmhd->bshd', p, v).reshape(b, s, MEM_WIDTH)


def setup_inputs(seed: int = 0) -> dict:
    key = jax.random.key(seed)
    ks = jax.random.split(key, 20)

    def w(k, shape, fan_in):
        return jax.random.normal(k, shape, jnp.float32) * fan_in ** -0.5

    def gain(k, shape):
        return 1.0 + 0.01 * jax.random.normal(k, shape, jnp.float32)

    return {
        "x": jax.random.normal(ks[0], (BATCH, SEQ, D_MODEL), jnp.float32),
        "mem": jax.random.normal(ks[1], (BATCH, MEM_TOKENS, D_MODEL), jnp.float32),
        "positions": jnp.broadcast_to(jnp.arange(SEQ, dtype=jnp.int32), (BATCH, SEQ)),
        "ffn1_norm": gain(ks[2], (DEPTH, D_MODEL)),
        "ffn1_w_gate_up": w(ks[3], (DEPTH, D_MODEL, 2 * D_FF), D_MODEL),
        "ffn1_w_down": w(ks[4], (DEPTH, D_FF, D_MODEL), D_FF),
        "mix_norm": gain(ks[5], (DEPTH, D_MODEL)),
        "w_in": w(ks[6], (DEPTH, D_MODEL, IN_WIDTH), D_MODEL),
        "q_norm": gain(ks[7], (DEPTH, HEAD_DIM)),
        "k_norm": gain(ks[8], (DEPTH, HEAD_DIM)),
        "conv_w": w(ks[9], (DEPTH, CONV_K, CONV_CH), CONV_K),
        "mem_norm": gain(ks[10], (DEPTH, D_MODEL)),
        "w_mem_kv": w(ks[11], (DEPTH, D_MODEL, 2 * MEM_WIDTH), D_MODEL),
        "mq_norm": gain(ks[12], (DEPTH, HEAD_DIM)),
        "mk_norm": gain(ks[13], (DEPTH, HEAD_DIM)),
        "w_out": w(ks[14], (DEPTH, MIX_WIDTH, D_MODEL), MIX_WIDTH),
        "ffn2_norm": gain(ks[15], (DEPTH, D_MODEL)),
        "ffn2_w_gate_up": w(ks[16], (DEPTH, D_MODEL, 2 * D_FF), D_MODEL),
        "ffn2_w_down": w(ks[17], (DEPTH, D_FF, D_MODEL), D_FF),
    }


def reference(x, mem, positions, ffn1_norm, ffn1_w_gate_up, ffn1_w_down, mix_norm, w_in,
              q_norm, k_norm, conv_w, mem_norm, w_mem_kv, mq_norm, mk_norm, w_out,
              ffn2_norm, ffn2_w_gate_up, ffn2_w_down):
    b, s, _ = x.shape
    cos, sin = rope_tables(positions)
    for i in range(DEPTH):
        x = x + FFN_RES * swiglu(rms_norm(x, ffn1_norm[i]), ffn1_w_gate_up[i], ffn1_w_down[i])
        h = rms_norm(x, mix_norm[i])
        proj = h @ w_in[i]
        q, k, v, c_b, c_c, c_x, m_q = jnp.split(proj, IN_SPLITS, axis=-1)
        q = apply_rope(rms_norm(q.reshape(b, s, N_ATTN_HEADS, HEAD_DIM), q_norm[i]), cos, sin)
        k = apply_rope(rms_norm(k.reshape(b, s, N_ATTN_HEADS, HEAD_DIM), k_norm[i]), cos, sin)
        v = v.reshape(b, s, N_ATTN_HEADS, HEAD_DIM)
        y_attn = moba_attention(q, k, v)
        y_conv = c_b * short_conv(c_c * c_x, conv_w[i])
        mkv = rms_norm(mem, mem_norm[i]) @ w_mem_kv[i]
        y_mem = memory_attention(m_q, mkv, mq_norm[i], mk_norm[i])
        x = x + jnp.concatenate([y_attn, y_conv, y_mem], axis=-1) @ w_out[i]
        x = x + FFN_RES * swiglu(rms_norm(x, ffn2_norm[i]), ffn2_w_gate_up[i], ffn2_w_down[i])
    return x
```

```python
import functools
import math

import jax
import jax.numpy as jnp
from jax import lax
from jax.experimental import pallas as pl
from jax.experimental.pallas import tpu as pltpu

F32 = jnp.float32
BF16 = jnp.bfloat16

D_MODEL = 2048
HEAD_DIM = 128
ATTN_WIDTH = 1024
N_ATTN_HEADS = 8
CONV_CH = 512
CONV_K = 3
MEM_WIDTH = 512
N_MEM_HEADS = 4
IN_WIDTH = 3 * ATTN_WIDTH + 3 * CONV_CH + MEM_WIDTH
D_FF = 5632
FFN_RES = 0.5
MOBA_BLOCK = 256
MOBA_TOPK = 3
ROPE_THETA = 10000.0
RMS_EPS = 1e-6

V7X_VMEM_BYTES = 64 * 1024 * 1024
VMEM_LIMIT = V7X_VMEM_BYTES - 8 * 1024 * 1024

MASK_BIAS = -1e30
AUG_DIM = 2 * HEAD_DIM
V_ROWS = HEAD_DIM + 16
LOG2E = math.log2(math.e)


def _rms(x, gain):
    return x * lax.rsqrt(jnp.mean(x * x, axis=-1, keepdims=True) + RMS_EPS) * gain


def _cparams(*sem):
    return pltpu.CompilerParams(dimension_semantics=sem, vmem_limit_bytes=VMEM_LIMIT)


def _resident(shape, index_map):
    return pl.BlockSpec(shape, index_map, pipeline_mode=pl.Buffered(1))


def _rope_kernel(pos_ref, inv_ref, cos_ref, sin_ref):
    ang = pos_ref[...].astype(F32) * inv_ref[...]
    lane = lax.broadcasted_iota(jnp.int32, ang.shape, 1)
    s = jnp.sin(ang)
    cos_ref[...] = jnp.cos(ang)
    sin_ref[...] = jnp.where(lane < HEAD_DIM // 2, -s, s)


def _rope_tables(positions):
    s = positions.shape[0]
    t = min(s, 2048)
    inv = ROPE_THETA ** (-jnp.arange(0, HEAD_DIM, 2, dtype=F32) / HEAD_DIM)
    inv = jnp.concatenate([inv, inv])[None, :]
    return pl.pallas_call(
        _rope_kernel,
        grid=(s // t,),
        in_specs=[pl.BlockSpec((t, 1), lambda i: (i, 0)),
                  pl.BlockSpec((1, HEAD_DIM), lambda i: (0, 0))],
        out_specs=[pl.BlockSpec((t, HEAD_DIM), lambda i: (i, 0))] * 2,
        out_shape=[jax.ShapeDtypeStruct((s, HEAD_DIM), F32)] * 2,
        compiler_params=_cparams("parallel"),
        name="rope_tables",
    )(positions[:, None], inv)


def _ffn_kernel(x_ref, g_ref, wg_ref, wu_ref, wd_ref, o_ref, h_ref):
    j = pl.program_id(1)

    @pl.when(j == 0)
    def _():
        h_ref[...] = _rms(x_ref[...], g_ref[...]).astype(BF16)

    h = h_ref[...]
    gate = jnp.dot(h, wg_ref[...], preferred_element_type=F32)
    up = jnp.dot(h, wu_ref[...], preferred_element_type=F32)
    act = (gate * jax.nn.sigmoid(gate) * up).astype(BF16)
    half_step = FFN_RES * jnp.dot(act, wd_ref[...], preferred_element_type=F32)

    @pl.when(j == 0)
    def _():
        o_ref[...] = x_ref[...] + half_step

    @pl.when(j > 0)
    def _():
        o_ref[...] += half_step


def _ffn(x, gain, w_gate_up, w_down, *, tm, tf):
    s = x.shape[0]
    n_f = D_FF // tf
    return pl.pallas_call(
        _ffn_kernel,
        grid=(s // tm, n_f),
        in_specs=[pl.BlockSpec((tm, D_MODEL), lambda i, j: (i, 0)),
                  pl.BlockSpec((1, D_MODEL), lambda i, j: (0, 0)),
                  pl.BlockSpec((D_MODEL, tf), lambda i, j: (0, j)),
                  pl.BlockSpec((D_MODEL, tf), lambda i, j: (0, j + n_f)),
                  pl.BlockSpec((tf, D_MODEL), lambda i, j: (j, 0))],
        out_specs=pl.BlockSpec((tm, D_MODEL), lambda i, j: (i, 0)),
        out_shape=jax.ShapeDtypeStruct((s, D_MODEL), F32),
        scratch_shapes=[pltpu.VMEM((tm, D_MODEL), BF16)],
        compiler_params=_cparams("parallel", "arbitrary"),
        name="ffn",
    )(x, gain, w_gate_up, w_gate_up, w_down)


def _memkv_kernel(mem_ref, g_ref, w_ref, kg_ref, mk_ref, mv_ref):
    h = _rms(mem_ref[...], g_ref[...]).astype(BF16)
    kv = jnp.dot(h, w_ref[...], preferred_element_type=F32)
    for hh in range(N_MEM_HEADS):
        sl = slice(hh * HEAD_DIM, (hh + 1) * HEAD_DIM)
        mk_ref[:, sl] = _rms(kv[:, sl], kg_ref[...]).astype(BF16)
    mv_ref[...] = kv[:, MEM_WIDTH:].astype(BF16)


def _memkv(mem, gain, w_mem_kv, k_gain):
    m = mem.shape[0]
    return pl.pallas_call(
        _memkv_kernel,
        out_shape=[jax.ShapeDtypeStruct((m, MEM_WIDTH), BF16)] * 2,
        compiler_params=pltpu.CompilerParams(vmem_limit_bytes=VMEM_LIMIT),
        name="memkv",
    )(mem, gain, w_mem_kv, k_gain)


def _mix_kernel(x_ref, g_ref, w_ref, qg_ref, kg_ref, cw_ref, cos_ref, sin_ref, mqg_ref,
                mk_ref, mv_ref, qT_ref, k_ref, vT_ref, ycm_ref, kmean_ref, carry_ref, *, tm):
    i = pl.program_id(0)
    blocks_per_tile = tm // MOBA_BLOCK
    n_blk = kmean_ref.shape[1]

    @pl.when(i == 0)
    def _():
        kmean_ref[...] = jnp.zeros_like(kmean_ref)
        carry_ref[...] = jnp.zeros_like(carry_ref)

    h = _rms(x_ref[...], g_ref[...]).astype(BF16)
    cos = cos_ref[...]
    sin = sin_ref[...]

    def rope(t):
        return t * cos + pltpu.roll(t, HEAD_DIM // 2, 1) * sin

    pk = jnp.dot(h, w_ref[:, ATTN_WIDTH:2 * ATTN_WIDTH], preferred_element_type=F32)
    slot = lax.broadcasted_iota(jnp.int32, (MOBA_BLOCK, HEAD_DIM), 1)
    for b in range(blocks_per_tile):
        one_hot = (slot == i * blocks_per_tile + b).astype(BF16)
        for hh in range(N_ATTN_HEADS):
            k_ref[hh, b, :, HEAD_DIM:] = one_hot
    for hh in range(N_ATTN_HEADS):
        kr = rope(_rms(pk[:, hh * HEAD_DIM:(hh + 1) * HEAD_DIM], kg_ref[...]))
        for b in range(blocks_per_tile):
            blk = kr[b * MOBA_BLOCK:(b + 1) * MOBA_BLOCK]
            k_ref[hh, b, :, :HEAD_DIM] = blk.astype(BF16)
            kmean_ref[hh, pl.ds(i * blocks_per_tile + b, 1), :] = jnp.mean(blk, axis=0, keepdims=True)

    pq = jnp.dot(h, w_ref[:, :ATTN_WIDTH], preferred_element_type=F32)
    q_blk = (i * tm + lax.broadcasted_iota(jnp.int32, (1, tm), 1)) // MOBA_BLOCK
    blk_id = lax.broadcasted_iota(jnp.int32, (n_blk, tm), 0)
    past = blk_id < q_blk
    for hh in range(N_ATTN_HEADS):
        qr = rope(_rms(pq[:, hh * HEAD_DIM:(hh + 1) * HEAD_DIM], qg_ref[...]))
        qT = qr.T
        gate = jnp.dot(kmean_ref[hh], qT, preferred_element_type=F32,
                       precision=lax.Precision.HIGHEST)
        gate = jnp.where(past, gate, -jnp.inf)
        selected = jnp.zeros(gate.shape, jnp.bool_)
        for _ in range(MOBA_TOPK):
            best = jnp.max(gate, axis=0, keepdims=True)
            first = jnp.min(jnp.where(gate == best, blk_id, n_blk), axis=0, keepdims=True)
            pick = (blk_id == first) & (best > -jnp.inf)
            selected = selected | pick
            gate = jnp.where(pick, -jnp.inf, gate)
        qT_ref[hh, :HEAD_DIM, :] = (qT * (HEAD_DIM ** -0.5 * LOG2E)).astype(BF16)
        qT_ref[hh, HEAD_DIM:HEAD_DIM + n_blk, :] = jnp.where(selected, 0.0, MASK_BIAS).astype(BF16)
        qT_ref[hh, HEAD_DIM + n_blk:, :] = jnp.zeros((AUG_DIM - HEAD_DIM - n_blk, tm), BF16)

    pv = jnp.dot(h, w_ref[:, 2 * ATTN_WIDTH:3 * ATTN_WIDTH], preferred_element_type=F32)
    ones_row = (lax.broadcasted_iota(jnp.int32, (V_ROWS - HEAD_DIM, MOBA_BLOCK), 0) == 0).astype(BF16)
    for hh in range(N_ATTN_HEADS):
        vT = pv[:, hh * HEAD_DIM:(hh + 1) * HEAD_DIM].T.astype(BF16)
        for b in range(blocks_per_tile):
            vT_ref[hh, b, :HEAD_DIM, :] = vT[:, b * MOBA_BLOCK:(b + 1) * MOBA_BLOCK]
            vT_ref[hh, b, HEAD_DIM:, :] = ones_row

    c0 = 3 * ATTN_WIDTH
    pc = jnp.dot(h, w_ref[:, c0:c0 + 3 * CONV_CH], preferred_element_type=F32)
    u = pc[:, CONV_CH:2 * CONV_CH] * pc[:, 2 * CONV_CH:]
    prev1 = carry_ref[7:8, :]
    prev2 = carry_ref[6:7, :]
    row = lax.broadcasted_iota(jnp.int32, u.shape, 0)
    u1 = jnp.where(row == 0, prev1, pltpu.roll(u, 1, 0))
    u2 = jnp.where(row == 0, prev2, jnp.where(row == 1, prev1, pltpu.roll(u, 2, 0)))
    carry_ref[...] = u[tm - 8:, :]
    conv = u2 * cw_ref[0:1, :] + u1 * cw_ref[1:2, :] + u * cw_ref[2:3, :]
    ycm_ref[:, :CONV_CH] = (pc[:, :CONV_CH] * conv).astype(BF16)

    pm = jnp.dot(h, w_ref[:, c0 + 3 * CONV_CH:], preferred_element_type=F32)
    for hh in range(N_MEM_HEADS):
        sl = slice(hh * HEAD_DIM, (hh + 1) * HEAD_DIM)
        mq = (_rms(pm[:, sl], mqg_ref[...]) * HEAD_DIM ** -0.5).astype(BF16)
        sc = lax.dot_general(mq, mk_ref[:, sl], (((1,), (1,)), ((), ())), preferred_element_type=F32)
        p = jnp.exp(sc - jnp.max(sc, axis=-1, keepdims=True))
        o = jnp.dot(p.astype(BF16), mv_ref[:, sl], preferred_element_type=F32)
        o = o / jnp.sum(p, axis=-1, keepdims=True)
        ycm_ref[:, CONV_CH + hh * HEAD_DIM:CONV_CH + (hh + 1) * HEAD_DIM] = o.astype(BF16)


def _mix(x, gain, w_in, q_gain, k_gain, conv_w, cos, sin, mq_gain, mk, mv, *, tm):
    s = x.shape[0]
    n_blk = s // MOBA_BLOCK
    bpt = tm // MOBA_BLOCK
    m = mk.shape[0]
    const = lambda i: (0, 0)
    return pl.pallas_call(
        functools.partial(_mix_kernel, tm=tm),
        grid=(s // tm,),
        in_specs=[pl.BlockSpec((tm, D_MODEL), lambda i: (i, 0)),
                  _resident((1, D_MODEL), const),
                  _resident((D_MODEL, IN_WIDTH), const),
                  _resident((1, HEAD_DIM), const),
                  _resident((1, HEAD_DIM), const),
                  _resident((CONV_K, CONV_CH), const),
                  pl.BlockSpec((tm, HEAD_DIM), lambda i: (i, 0)),
                  pl.BlockSpec((tm, HEAD_DIM), lambda i: (i, 0)),
                  _resident((1, HEAD_DIM), const),
                  _resident((m, MEM_WIDTH), const),
                  _resident((m, MEM_WIDTH), const)],
        out_specs=[pl.BlockSpec((N_ATTN_HEADS, AUG_DIM, tm), lambda i: (0, 0, i)),
                   pl.BlockSpec((N_ATTN_HEADS, bpt, MOBA_BLOCK, AUG_DIM), lambda i: (0, i, 0, 0)),
                   pl.BlockSpec((N_ATTN_HEADS, bpt, V_ROWS, MOBA_BLOCK), lambda i: (0, i, 0, 0)),
                   pl.BlockSpec((tm, CONV_CH + MEM_WIDTH), lambda i: (i, 0))],
        out_shape=[jax.ShapeDtypeStruct((N_ATTN_HEADS, AUG_DIM, s), BF16),
                   jax.ShapeDtypeStruct((N_ATTN_HEADS, n_blk, MOBA_BLOCK, AUG_DIM), BF16),
                   jax.ShapeDtypeStruct((N_ATTN_HEADS, n_blk, V_ROWS, MOBA_BLOCK), BF16),
                   jax.ShapeDtypeStruct((s, CONV_CH + MEM_WIDTH), BF16)],
        scratch_shapes=[pltpu.VMEM((N_ATTN_HEADS, n_blk, HEAD_DIM), F32),
                        pltpu.VMEM((8, CONV_CH), F32)],
        compiler_params=_cparams("arbitrary"),
        name="mix_proj",
    )(x, gain, w_in, q_gain, k_gain, conv_w, cos, sin, mq_gain, mk, mv)


def _attn_kernel(qT_ref, k_ref, vT_ref, o_ref):
    i = pl.program_id(1)
    qT = qT_ref[0]

    s = jnp.dot(k_ref[0, i, :, :HEAD_DIM], qT[:HEAD_DIM], preferred_element_type=F32)
    key_pos = lax.broadcasted_iota(jnp.int32, s.shape, 0)
    qry_pos = lax.broadcasted_iota(jnp.int32, s.shape, 1)
    s = jnp.where(key_pos <= qry_pos, s, MASK_BIAS)
    m0 = jnp.max(s, axis=0, keepdims=True)
    p = jnp.exp2(s - m0).astype(BF16)
    acc0 = jnp.dot(vT_ref[0, i], p, preferred_element_type=F32)

    def body(j, carry):
        m, acc = carry
        s = jnp.dot(k_ref[0, j], qT, preferred_element_type=F32)
        m_new = jnp.maximum(m, jnp.max(s, axis=0, keepdims=True))
        p = jnp.exp2(s - m_new).astype(BF16)
        acc = acc * jnp.exp2(m - m_new) + jnp.dot(vT_ref[0, j], p, preferred_element_type=F32)
        return m_new, acc

    _, acc = lax.fori_loop(0, i, body, (m0, acc0))
    out = acc[:HEAD_DIM] / acc[HEAD_DIM:HEAD_DIM + 1]
    o_ref[...] = out.T.astype(o_ref.dtype)


def _attention(qT, k, vT):
    n_blk = k.shape[1]
    s = n_blk * MOBA_BLOCK
    return pl.pallas_call(
        _attn_kernel,
        grid=(N_ATTN_HEADS, n_blk),
        in_specs=[pl.BlockSpec((1, AUG_DIM, MOBA_BLOCK), lambda h, i: (h, 0, i)),
                  pl.BlockSpec((1, n_blk, MOBA_BLOCK, AUG_DIM), lambda h, i: (h, 0, 0, 0)),
                  pl.BlockSpec((1, n_blk, V_ROWS, MOBA_BLOCK), lambda h, i: (h, 0, 0, 0))],
        out_specs=pl.BlockSpec((MOBA_BLOCK, HEAD_DIM), lambda h, i: (i, h)),
        out_shape=jax.ShapeDtypeStruct((s, ATTN_WIDTH), BF16),
        compiler_params=_cparams("parallel", "arbitrary"),
        name="moba_attn",
    )(qT, k, vT)


def _outproj_kernel(x_ref, ya_ref, ycm_ref, w_ref, o_ref):
    o_ref[...] = (x_ref[...]
                  + jnp.dot(ya_ref[...], w_ref[:ATTN_WIDTH], preferred_element_type=F32)
                  + jnp.dot(ycm_ref[...], w_ref[ATTN_WIDTH:], preferred_element_type=F32))


def _outproj(x, y_attn, y_cm, w_out, *, tm):
    s = x.shape[0]
    return pl.pallas_call(
        _outproj_kernel,
        grid=(s // tm,),
        in_specs=[pl.BlockSpec((tm, D_MODEL), lambda i: (i, 0)),
                  pl.BlockSpec((tm, ATTN_WIDTH), lambda i: (i, 0)),
                  pl.BlockSpec((tm, CONV_CH + MEM_WIDTH), lambda i: (i, 0)),
                  _resident((D_MODEL, D_MODEL), lambda i: (0, 0))],
        out_specs=pl.BlockSpec((tm, D_MODEL), lambda i: (i, 0)),
        out_shape=jax.ShapeDtypeStruct((s, D_MODEL), F32),
        compiler_params=_cparams("parallel"),
        name="out_proj",
    )(x, y_attn, y_cm, w_out)


def kernel(x, mem, positions, ffn1_norm, ffn1_w_gate_up, ffn1_w_down, mix_norm, w_in, q_norm, k_norm,
           conv_w, mem_norm, w_mem_kv, mq_norm, mk_norm, w_out, ffn2_norm, ffn2_w_gate_up, ffn2_w_down):
    b, s, _ = x.shape
    assert b == 1 and s % MOBA_BLOCK == 0 and s // MOBA_BLOCK <= HEAD_DIM // 2
    depth = w_in.shape[0]
    tm = min(s, 512)
    ffn = functools.partial(_ffn, tm=tm, tf=512)

    cos, sin = _rope_tables(positions[0])
    xs = x[0]
    mem2 = mem[0]
    for l in range(depth):
        xs = ffn(xs, ffn1_norm[l][None], ffn1_w_gate_up[l].astype(BF16), ffn1_w_down[l].astype(BF16))
        mk, mv = _memkv(mem2, mem_norm[l][None], w_mem_kv[l].astype(BF16), mk_norm[l][None])
        qT, k, vT, y_cm = _mix(xs, mix_norm[l][None], w_in[l].astype(BF16), q_norm[l][None],
                               k_norm[l][None], conv_w[l], cos, sin, mq_norm[l][None], mk, mv, tm=tm)
        y_attn = _attention(qT, k, vT)
        xs = _outproj(xs, y_attn, y_cm, w_out[l].astype(BF16), tm=tm)
        xs = ffn(xs, ffn2_norm[l][None], ffn2_w_gate_up[l].astype(BF16), ffn2_w_down[l].astype(BF16))
    return xs[None]
```

```python
import functools
import math

import jax
import jax.numpy as jnp
from jax import lax
from jax.experimental import pallas as pl
from jax.experimental.pallas import tpu as pltpu

F32 = jnp.float32
BF16 = jnp.bfloat16

D_MODEL = 2048
HEAD_DIM = 128
ATTN_WIDTH = 1024
N_ATTN_HEADS = 8
CONV_CH = 512
CONV_K = 3
MEM_WIDTH = 512
N_MEM_HEADS = 4
IN_WIDTH = 3 * ATTN_WIDTH + 3 * CONV_CH + MEM_WIDTH
D_FF = 5632
FFN_RES = 0.5
MOBA_BLOCK = 256
MOBA_TOPK = 3
ROPE_THETA = 10000.0
RMS_EPS = 1e-6

V7X_VMEM_BYTES = 64 * 1024 * 1024
VMEM_LIMIT = V7X_VMEM_BYTES - 8 * 1024 * 1024

MASK_BIAS = -1e30
AUG_DIM = 2 * HEAD_DIM
V_ROWS = HEAD_DIM + 16
LOG2E = math.log2(math.e)


def _rms(x, gain):
    return x * lax.rsqrt(jnp.mean(x * x, axis=-1, keepdims=True) + RMS_EPS) * gain


def _cparams(*sem):
    return pltpu.CompilerParams(dimension_semantics=sem, vmem_limit_bytes=VMEM_LIMIT)


def _resident(shape, index_map):
    return pl.BlockSpec(shape, index_map, pipeline_mode=pl.Buffered(1))


def _rope_kernel(pos_ref, inv_ref, cos_ref, sin_ref):
    ang = pos_ref[...].astype(F32) * inv_ref[...]
    lane = lax.broadcasted_iota(jnp.int32, ang.shape, 1)
    s = jnp.sin(ang)
    cos_ref[...] = jnp.cos(ang)
    sin_ref[...] = jnp.where(lane < HEAD_DIM // 2, -s, s)


def _rope_tables(positions):
    s = positions.shape[0]
    t = min(s, 2048)
    inv = ROPE_THETA ** (-jnp.arange(0, HEAD_DIM, 2, dtype=F32) / HEAD_DIM)
    inv = jnp.concatenate([inv, inv])[None, :]
    return pl.pallas_call(
        _rope_kernel,
        grid=(s // t,),
        in_specs=[pl.BlockSpec((t, 1), lambda i: (i, 0)),
                  pl.BlockSpec((1, HEAD_DIM), lambda i: (0, 0))],
        out_specs=[pl.BlockSpec((t, HEAD_DIM), lambda i: (i, 0))] * 2,
        out_shape=[jax.ShapeDtypeStruct((s, HEAD_DIM), F32)] * 2,
        compiler_params=_cparams("parallel"),
        name="rope_tables",
    )(positions[:, None], inv)


def _ffn_kernel(x_ref, g_ref, wg_ref, wu_ref, wd_ref, o_ref, h_ref):
    j = pl.program_id(1)

    @pl.when(j == 0)
    def _():
        h_ref[...] = _rms(x_ref[...], g_ref[...]).astype(BF16)

    h = h_ref[...]
    gate = jnp.dot(h, wg_ref[...], preferred_element_type=F32)
    up = jnp.dot(h, wu_ref[...], preferred_element_type=F32)
    act = (gate * jax.nn.sigmoid(gate) * up).astype(BF16)
    half_step = FFN_RES * jnp.dot(act, wd_ref[...], preferred_element_type=F32)

    @pl.when(j == 0)
    def _():
        o_ref[...] = x_ref[...] + half_step

    @pl.when(j > 0)
    def _():
        o_ref[...] += half_step


def _ffn(x, gain, w_gate_up, w_down, *, tm, tf):
    s = x.shape[0]
    n_f = D_FF // tf
    return pl.pallas_call(
        _ffn_kernel,
        grid=(s // tm, n_f),
        in_specs=[pl.BlockSpec((tm, D_MODEL), lambda i, j: (i, 0)),
                  pl.BlockSpec((1, D_MODEL), lambda i, j: (0, 0)),
                  pl.BlockSpec((D_MODEL, tf), lambda i, j: (0, j)),
                  pl.BlockSpec((D_MODEL, tf), lambda i, j: (0, j + n_f)),
                  pl.BlockSpec((tf, D_MODEL), lambda i, j: (j, 0))],
        out_specs=pl.BlockSpec((tm, D_MODEL), lambda i, j: (i, 0)),
        out_shape=jax.ShapeDtypeStruct((s, D_MODEL), F32),
        scratch_shapes=[pltpu.VMEM((tm, D_MODEL), BF16)],
        compiler_params=_cparams("parallel", "arbitrary"),
        name="ffn",
    )(x, gain, w_gate_up, w_gate_up, w_down)


def _memkv_kernel(mem_ref, g_ref, w_ref, kg_ref, mk_ref, mv_ref):
    h = _rms(mem_ref[...], g_ref[...]).astype(BF16)
    kv = jnp.dot(h, w_ref[...], preferred_element_type=F32)
    for hh in range(N_MEM_HEADS):
        sl = slice(hh * HEAD_DIM, (hh + 1) * HEAD_DIM)
        mk_ref[:, sl] = _rms(kv[:, sl], kg_ref[...]).astype(BF16)
    mv_ref[...] = kv[:, MEM_WIDTH:].astype(BF16)


def _memkv(mem, gain, w_mem_kv, k_gain):
    m = mem.shape[0]
    return pl.pallas_call(
        _memkv_kernel,
        out_shape=[jax.ShapeDtypeStruct((m, MEM_WIDTH), BF16)] * 2,
        compiler_params=pltpu.CompilerParams(vmem_limit_bytes=VMEM_LIMIT),
        name="memkv",
    )(mem, gain, w_mem_kv, k_gain)


def _mix_kernel(x_ref, g_ref, w_ref, qg_ref, kg_ref, cw_ref, cos_ref, sin_ref, mqg_ref,
                mk_ref, mv_ref, qT_ref, k_ref, vT_ref, ycm_ref, kmean_ref, carry_ref, *, tm):
    i = pl.program_id(0)
    blocks_per_tile = tm // MOBA_BLOCK
    n_blk = kmean_ref.shape[1]

    @pl.when(i == 0)
    def _():
        kmean_ref[...] = jnp.zeros_like(kmean_ref)
        carry_ref[...] = jnp.zeros_like(carry_ref)

    h = _rms(x_ref[...], g_ref[...]).astype(BF16)
    cos = cos_ref[...]
    sin = sin_ref[...]

    def rope(t):
        return t * cos + pltpu.roll(t, HEAD_DIM // 2, 1) * sin

    pk = jnp.dot(h, w_ref[:, ATTN_WIDTH:2 * ATTN_WIDTH], preferred_element_type=F32)
    slot = lax.broadcasted_iota(jnp.int32, (MOBA_BLOCK, HEAD_DIM), 1)
    for b in range(blocks_per_tile):
        one_hot = (slot == i * blocks_per_tile + b).astype(BF16)
        for hh in range(N_ATTN_HEADS):
            k_ref[hh, b, :, HEAD_DIM:] = one_hot
    for hh in range(N_ATTN_HEADS):
        kr = rope(_rms(pk[:, hh * HEAD_DIM:(hh + 1) * HEAD_DIM], kg_ref[...]))
        for b in range(blocks_per_tile):
            blk = kr[b * MOBA_BLOCK:(b + 1) * MOBA_BLOCK]
            k_ref[hh, b, :, :HEAD_DIM] = blk.astype(BF16)
            kmean_ref[hh, pl.ds(i * blocks_per_tile + b, 1), :] = jnp.mean(blk, axis=0, keepdims=True)

    pq = jnp.dot(h, w_ref[:, :ATTN_WIDTH], preferred_element_type=F32)
    q_blk = (i * tm + lax.broadcasted_iota(jnp.int32, (1, tm), 1)) // MOBA_BLOCK
    blk_id = lax.broadcasted_iota(jnp.int32, (n_blk, tm), 0)
    past = blk_id < q_blk
    for hh in range(N_ATTN_HEADS):
        qr = rope(_rms(pq[:, hh * HEAD_DIM:(hh + 1) * HEAD_DIM], qg_ref[...]))
        qT = qr.T
        gate = jnp.dot(kmean_ref[hh], qT, preferred_element_type=F32,
                       precision=lax.Precision.HIGHEST)
        gate = jnp.where(past, gate, -jnp.inf)
        selected = blk_id == q_blk
        for _ in range(MOBA_TOPK):
            best = jnp.max(gate, axis=0, keepdims=True)
            first = jnp.min(jnp.where(gate == best, blk_id, n_blk), axis=0, keepdims=True)
            pick = (blk_id == first) & (best > -jnp.inf)
            selected = selected | pick
            gate = jnp.where(pick, -jnp.inf, gate)
        qT_ref[hh, :HEAD_DIM, :] = (qT * (HEAD_DIM ** -0.5 * LOG2E)).astype(BF16)
        qT_ref[hh, HEAD_DIM:HEAD_DIM + n_blk, :] = jnp.where(selected, 0.0, MASK_BIAS).astype(BF16)
        qT_ref[hh, HEAD_DIM + n_blk:, :] = jnp.zeros((AUG_DIM - HEAD_DIM - n_blk, tm), BF16)

    pv = jnp.dot(h, w_ref[:, 2 * ATTN_WIDTH:3 * ATTN_WIDTH], preferred_element_type=F32)
    ones_row = (lax.broadcasted_iota(jnp.int32, (V_ROWS - HEAD_DIM, MOBA_BLOCK), 0) == 0).astype(BF16)
    for hh in range(N_ATTN_HEADS):
        vT = pv[:, hh * HEAD_DIM:(hh + 1) * HEAD_DIM].T.astype(BF16)
        for b in range(blocks_per_tile):
            vT_ref[hh, b, :HEAD_DIM, :] = vT[:, b * MOBA_BLOCK:(b + 1) * MOBA_BLOCK]
            vT_ref[hh, b, HEAD_DIM:, :] = ones_row

    c0 = 3 * ATTN_WIDTH
    pc = jnp.dot(h, w_ref[:, c0:c0 + 3 * CONV_CH], preferred_element_type=F32)
    u = pc[:, CONV_CH:2 * CONV_CH] * pc[:, 2 * CONV_CH:]
    prev1 = carry_ref[7:8, :]
    prev2 = carry_ref[6:7, :]
    row = lax.broadcasted_iota(jnp.int32, u.shape, 0)
    u1 = jnp.where(row == 0, prev1, pltpu.roll(u, 1, 0))
    u2 = jnp.where(row == 0, prev2, jnp.where(row == 1, prev1, pltpu.roll(u, 2, 0)))
    carry_ref[...] = u[tm - 8:, :]
    conv = u2 * cw_ref[0:1, :] + u1 * cw_ref[1:2, :] + u * cw_ref[2:3, :]
    ycm_ref[:, :CONV_CH] = (pc[:, :CONV_CH] * conv).astype(BF16)

    pm = jnp.dot(h, w_ref[:, c0 + 3 * CONV_CH:], preferred_element_type=F32)
    for hh in range(N_MEM_HEADS):
        sl = slice(hh * HEAD_DIM, (hh + 1) * HEAD_DIM)
        mq = (_rms(pm[:, sl], mqg_ref[...]) * HEAD_DIM ** -0.5).astype(BF16)
        sc = lax.dot_general(mq, mk_ref[:, sl], (((1,), (1,)), ((), ())), preferred_element_type=F32)
        p = jnp.exp(sc - jnp.max(sc, axis=-1, keepdims=True))
        o = jnp.dot(p.astype(BF16), mv_ref[:, sl], preferred_element_type=F32)
        o = o / jnp.sum(p, axis=-1, keepdims=True)
        ycm_ref[:, CONV_CH + hh * HEAD_DIM:CONV_CH + (hh + 1) * HEAD_DIM] = o.astype(BF16)


def _mix(x, gain, w_in, q_gain, k_gain, conv_w, cos, sin, mq_gain, mk, mv, *, tm):
    s = x.shape[0]
    n_blk = s // MOBA_BLOCK
    bpt = tm // MOBA_BLOCK
    m = mk.shape[0]
    const = lambda i: (0, 0)
    return pl.pallas_call(
        functools.partial(_mix_kernel, tm=tm),
        grid=(s // tm,),
        in_specs=[pl.BlockSpec((tm, D_MODEL), lambda i: (i, 0)),
                  _resident((1, D_MODEL), const),
                  _resident((D_MODEL, IN_WIDTH), const),
                  _resident((1, HEAD_DIM), const),
                  _resident((1, HEAD_DIM), const),
                  _resident((CONV_K, CONV_CH), const),
                  pl.BlockSpec((tm, HEAD_DIM), lambda i: (i, 0)),
                  pl.BlockSpec((tm, HEAD_DIM), lambda i: (i, 0)),
                  _resident((1, HEAD_DIM), const),
                  _resident((m, MEM_WIDTH), const),
                  _resident((m, MEM_WIDTH), const)],
        out_specs=[pl.BlockSpec((N_ATTN_HEADS, AUG_DIM, tm), lambda i: (0, 0, i)),
                   pl.BlockSpec((N_ATTN_HEADS, bpt, MOBA_BLOCK, AUG_DIM), lambda i: (0, i, 0, 0)),
                   pl.BlockSpec((N_ATTN_HEADS, bpt, V_ROWS, MOBA_BLOCK), lambda i: (0, i, 0, 0)),
                   pl.BlockSpec((tm, CONV_CH + MEM_WIDTH), lambda i: (i, 0))],
        out_shape=[jax.ShapeDtypeStruct((N_ATTN_HEADS, AUG_DIM, s), BF16),
                   jax.ShapeDtypeStruct((N_ATTN_HEADS, n_blk, MOBA_BLOCK, AUG_DIM), BF16),
                   jax.ShapeDtypeStruct((N_ATTN_HEADS, n_blk, V_ROWS, MOBA_BLOCK), BF16),
                   jax.ShapeDtypeStruct((s, CONV_CH + MEM_WIDTH), BF16)],
        scratch_shapes=[pltpu.VMEM((N_ATTN_HEADS, n_blk, HEAD_DIM), F32),
                        pltpu.VMEM((8, CONV_CH), F32)],
        compiler_params=_cparams("arbitrary"),
        name="mix_proj",
    )(x, gain, w_in, q_gain, k_gain, conv_w, cos, sin, mq_gain, mk, mv)


Q_BLOCKS = 4
M_INIT = 2 * MASK_BIAS


def _attn_kernel(qT_ref, k_ref, vT_ref, o_ref, acc_ref, *, q_blocks):
    t = pl.program_id(1)
    qT = qT_ref[0]
    acc_ref[...] = jnp.zeros_like(acc_ref)

    def step(j, m, causal_block=None):
        s = jnp.dot(k_ref[0, j], qT, preferred_element_type=F32)
        if causal_block is not None:
            lo, hi = causal_block * MOBA_BLOCK, (causal_block + 1) * MOBA_BLOCK
            own = s[:, lo:hi]
            key_pos = lax.broadcasted_iota(jnp.int32, own.shape, 0)
            qry_pos = lax.broadcasted_iota(jnp.int32, own.shape, 1)
            own = jnp.where(key_pos <= qry_pos, own, MASK_BIAS)
            pieces = ([s[:, :lo]] if lo else []) + [own] + ([s[:, hi:]] if hi < s.shape[1] else [])
            s = jnp.concatenate(pieces, axis=1)
        m_new = jnp.maximum(m, jnp.max(s, axis=0, keepdims=True))
        p = jnp.exp2(s - m_new).astype(BF16)
        acc_ref[...] = (acc_ref[...] * jnp.exp2(m - m_new)
                        + jnp.dot(vT_ref[0, j], p, preferred_element_type=F32))
        return m_new

    def past_blocks(u, m):
        for d in range(q_blocks):
            m = step(u * q_blocks + d, m)
        return m

    m = jnp.full((1, q_blocks * MOBA_BLOCK), M_INIT, F32)
    m = lax.fori_loop(0, t, past_blocks, m)
    for d in range(q_blocks):
        m = step(t * q_blocks + d, m, causal_block=d)
    out = acc_ref[:HEAD_DIM, :] / acc_ref[HEAD_DIM:HEAD_DIM + 1, :]
    o_ref[...] = out.T.astype(o_ref.dtype)


def _attention(qT, k, vT):
    n_blk = k.shape[1]
    s = n_blk * MOBA_BLOCK
    q_blocks = math.gcd(Q_BLOCKS, n_blk)
    q_tile = q_blocks * MOBA_BLOCK
    return pl.pallas_call(
        functools.partial(_attn_kernel, q_blocks=q_blocks),
        grid=(N_ATTN_HEADS, s // q_tile),
        in_specs=[pl.BlockSpec((1, AUG_DIM, q_tile), lambda h, t: (h, 0, t)),
                  pl.BlockSpec((1, n_blk, MOBA_BLOCK, AUG_DIM), lambda h, t: (h, 0, 0, 0)),
                  pl.BlockSpec((1, n_blk, V_ROWS, MOBA_BLOCK), lambda h, t: (h, 0, 0, 0))],
        out_specs=pl.BlockSpec((q_tile, HEAD_DIM), lambda h, t: (t, h)),
        out_shape=jax.ShapeDtypeStruct((s, ATTN_WIDTH), BF16),
        scratch_shapes=[pltpu.VMEM((V_ROWS, q_tile), F32)],
        compiler_params=_cparams("parallel", "arbitrary"),
        name="moba_attn",
    )(qT, k, vT)


def _outproj_kernel(x_ref, ya_ref, ycm_ref, w_ref, o_ref):
    o_ref[...] = (x_ref[...]
                  + jnp.dot(ya_ref[...], w_ref[:ATTN_WIDTH], preferred_element_type=F32)
                  + jnp.dot(ycm_ref[...], w_ref[ATTN_WIDTH:], preferred_element_type=F32))


def _outproj(x, y_attn, y_cm, w_out, *, tm):
    s = x.shape[0]
    return pl.pallas_call(
        _outproj_kernel,
        grid=(s // tm,),
        in_specs=[pl.BlockSpec((tm, D_MODEL), lambda i: (i, 0)),
                  pl.BlockSpec((tm, ATTN_WIDTH), lambda i: (i, 0)),
                  pl.BlockSpec((tm, CONV_CH + MEM_WIDTH), lambda i: (i, 0)),
                  _resident((D_MODEL, D_MODEL), lambda i: (0, 0))],
        out_specs=pl.BlockSpec((tm, D_MODEL), lambda i: (i, 0)),
        out_shape=jax.ShapeDtypeStruct((s, D_MODEL), F32),
        compiler_params=_cparams("parallel"),
        name="out_proj",
    )(x, y_attn, y_cm, w_out)


def kernel(x, mem, positions, ffn1_norm, ffn1_w_gate_up, ffn1_w_down, mix_norm, w_in, q_norm, k_norm,
           conv_w, mem_norm, w_mem_kv, mq_norm, mk_norm, w_out, ffn2_norm, ffn2_w_gate_up, ffn2_w_down):
    b, s, _ = x.shape
    assert b == 1 and s % MOBA_BLOCK == 0 and s // MOBA_BLOCK <= HEAD_DIM // 2
    depth = w_in.shape[0]
    tm = min(s, 512)
    ffn = functools.partial(_ffn, tm=tm, tf=512)

    cos, sin = _rope_tables(positions[0])
    xs = x[0]
    mem2 = mem[0]
    for l in range(depth):
        xs = ffn(xs, ffn1_norm[l][None], ffn1_w_gate_up[l].astype(BF16), ffn1_w_down[l].astype(BF16))
        mk, mv = _memkv(mem2, mem_norm[l][None], w_mem_kv[l].astype(BF16), mk_norm[l][None])
        qT, k, vT, y_cm = _mix(xs, mix_norm[l][None], w_in[l].astype(BF16), q_norm[l][None],
                               k_norm[l][None], conv_w[l], cos, sin, mq_norm[l][None], mk, mv, tm=tm)
        y_attn = _attention(qT, k, vT)
        xs = _outproj(xs, y_attn, y_cm, w_out[l].astype(BF16), tm=tm)
        xs = ffn(xs, ffn2_norm[l][None], ffn2_w_gate_up[l].astype(BF16), ffn2_w_down[l].astype(BF16))
    return xs[None]
```

```python
import functools
import math

import jax
import jax.numpy as jnp
from jax import lax
from jax.experimental import pallas as pl
from jax.experimental.pallas import tpu as pltpu

F32 = jnp.float32
BF16 = jnp.bfloat16

D_MODEL = 2048
HEAD_DIM = 128
ATTN_WIDTH = 1024
N_ATTN_HEADS = 8
CONV_CH = 512
CONV_K = 3
MEM_WIDTH = 512
N_MEM_HEADS = 4
IN_WIDTH = 3 * ATTN_WIDTH + 3 * CONV_CH + MEM_WIDTH
D_FF = 5632
FFN_RES = 0.5
MOBA_BLOCK = 256
MOBA_TOPK = 3
ROPE_THETA = 10000.0
RMS_EPS = 1e-6

V7X_VMEM_BYTES = 64 * 1024 * 1024
VMEM_LIMIT = V7X_VMEM_BYTES - 8 * 1024 * 1024

MASK_BIAS = -1e30
AUG_DIM = 2 * HEAD_DIM
V_ROWS = HEAD_DIM + 16
LOG2E = math.log2(math.e)


def _rms(x, gain):
    return x * lax.rsqrt(jnp.mean(x * x, axis=-1, keepdims=True) + RMS_EPS) * gain


def _cparams(*sem):
    return pltpu.CompilerParams(dimension_semantics=sem, vmem_limit_bytes=VMEM_LIMIT)


def _resident(shape, index_map):
    return pl.BlockSpec(shape, index_map, pipeline_mode=pl.Buffered(1))


def _rope_kernel(pos_ref, inv_ref, cos_ref, sin_ref):
    ang = pos_ref[...].astype(F32) * inv_ref[...]
    lane = lax.broadcasted_iota(jnp.int32, ang.shape, 1)
    s = jnp.sin(ang)
    cos_ref[...] = jnp.cos(ang)
    sin_ref[...] = jnp.where(lane < HEAD_DIM // 2, -s, s)


def _rope_tables(positions):
    s = positions.shape[0]
    t = min(s, 2048)
    inv = ROPE_THETA ** (-jnp.arange(0, HEAD_DIM, 2, dtype=F32) / HEAD_DIM)
    inv = jnp.concatenate([inv, inv])[None, :]
    return pl.pallas_call(
        _rope_kernel,
        grid=(s // t,),
        in_specs=[pl.BlockSpec((t, 1), lambda i: (i, 0)),
                  pl.BlockSpec((1, HEAD_DIM), lambda i: (0, 0))],
        out_specs=[pl.BlockSpec((t, HEAD_DIM), lambda i: (i, 0))] * 2,
        out_shape=[jax.ShapeDtypeStruct((s, HEAD_DIM), F32)] * 2,
        compiler_params=_cparams("parallel"),
        name="rope_tables",
    )(positions[:, None], inv)


def _ffn_kernel(x_ref, g_ref, wg_ref, wu_ref, wd_ref, o_ref, h_ref):
    j = pl.program_id(1)

    @pl.when(j == 0)
    def _():
        x = x_ref[...]
        h_ref[...] = _rms(x, g_ref[...]).astype(BF16)
        o_ref[...] = x

    h = h_ref[...]
    gate = jnp.dot(h, wg_ref[...], preferred_element_type=F32)
    up = jnp.dot(h, wu_ref[...], preferred_element_type=F32)
    act = (gate * jax.nn.sigmoid(gate) * up * FFN_RES).astype(BF16)
    o_ref[...] += jnp.dot(act, wd_ref[...], preferred_element_type=F32)


def _ffn(x, gain, w_gate_up, w_down, layer, *, tm, tf):
    s = x.shape[0]
    n_f = D_FF // tf
    return pl.pallas_call(
        _ffn_kernel,
        grid=(s // tm, n_f),
        in_specs=[pl.BlockSpec((tm, D_MODEL), lambda i, j: (i, 0)),
                  pl.BlockSpec((1, D_MODEL), lambda i, j: (0, 0)),
                  pl.BlockSpec((None, D_MODEL, tf), lambda i, j: (layer, 0, j)),
                  pl.BlockSpec((None, D_MODEL, tf), lambda i, j: (layer, 0, j + n_f)),
                  pl.BlockSpec((None, tf, D_MODEL), lambda i, j: (layer, j, 0))],
        out_specs=pl.BlockSpec((tm, D_MODEL), lambda i, j: (i, 0)),
        out_shape=jax.ShapeDtypeStruct((s, D_MODEL), F32),
        scratch_shapes=[pltpu.VMEM((tm, D_MODEL), BF16)],
        compiler_params=_cparams("parallel", "arbitrary"),
        name="ffn",
    )(x, gain, w_gate_up, w_gate_up, w_down)


def _memkv_kernel(mem_ref, g_ref, w_ref, kg_ref, mk_ref, mv_ref):
    h = _rms(mem_ref[...], g_ref[...]).astype(BF16)
    kv = jnp.dot(h, w_ref[...], preferred_element_type=F32)
    for hh in range(N_MEM_HEADS):
        sl = slice(hh * HEAD_DIM, (hh + 1) * HEAD_DIM)
        mk_ref[:, sl] = _rms(kv[:, sl], kg_ref[...]).astype(BF16)
    mv_ref[...] = kv[:, MEM_WIDTH:].astype(BF16)


def _memkv(mem, gain, w_mem_kv, k_gain, layer):
    m = mem.shape[0]
    const = lambda i: (0, 0)
    return pl.pallas_call(
        _memkv_kernel,
        grid=(1,),
        in_specs=[pl.BlockSpec((m, D_MODEL), const),
                  pl.BlockSpec((1, D_MODEL), const),
                  pl.BlockSpec((None, D_MODEL, 2 * MEM_WIDTH), lambda i: (layer, 0, 0)),
                  pl.BlockSpec((1, HEAD_DIM), const)],
        out_specs=[pl.BlockSpec((m, MEM_WIDTH), const)] * 2,
        out_shape=[jax.ShapeDtypeStruct((m, MEM_WIDTH), BF16)] * 2,
        compiler_params=_cparams("arbitrary"),
        name="memkv",
    )(mem, gain, w_mem_kv, k_gain)


def _mix_kernel(x_ref, g_ref, w_ref, qg_ref, kg_ref, cw_ref, cos_ref, sin_ref, mqg_ref,
                mk_ref, mv_ref, qT_ref, k_ref, vT_ref, ycm_ref, kmean_ref, carry_ref, *, tm):
    i = pl.program_id(0)
    blocks_per_tile = tm // MOBA_BLOCK
    n_blk = kmean_ref.shape[1]

    @pl.when(i == 0)
    def _():
        kmean_ref[...] = jnp.zeros_like(kmean_ref)
        carry_ref[...] = jnp.zeros_like(carry_ref)

    h = _rms(x_ref[...], g_ref[...]).astype(BF16)
    cos = cos_ref[...]
    sin = sin_ref[...]

    def rope(t):
        return t * cos + pltpu.roll(t, HEAD_DIM // 2, 1) * sin

    pk = jnp.dot(h, w_ref[:, ATTN_WIDTH:2 * ATTN_WIDTH], preferred_element_type=F32)
    slot = lax.broadcasted_iota(jnp.int32, (MOBA_BLOCK, HEAD_DIM), 1)
    for b in range(blocks_per_tile):
        one_hot = (slot == i * blocks_per_tile + b).astype(BF16)
        for hh in range(N_ATTN_HEADS):
            k_ref[hh, b, :, HEAD_DIM:] = one_hot
    for hh in range(N_ATTN_HEADS):
        kr = rope(_rms(pk[:, hh * HEAD_DIM:(hh + 1) * HEAD_DIM], kg_ref[...]))
        for b in range(blocks_per_tile):
            blk = kr[b * MOBA_BLOCK:(b + 1) * MOBA_BLOCK]
            k_ref[hh, b, :, :HEAD_DIM] = blk.astype(BF16)
            kmean_ref[hh, pl.ds(i * blocks_per_tile + b, 1), :] = jnp.mean(blk, axis=0, keepdims=True)

    pq = jnp.dot(h, w_ref[:, :ATTN_WIDTH], preferred_element_type=F32)
    q_blk = (i * tm + lax.broadcasted_iota(jnp.int32, (1, tm), 1)) // MOBA_BLOCK
    blk_id = lax.broadcasted_iota(jnp.int32, (n_blk, tm), 0)
    past = blk_id < q_blk
    for hh in range(N_ATTN_HEADS):
        qr = rope(_rms(pq[:, hh * HEAD_DIM:(hh + 1) * HEAD_DIM], qg_ref[...]))
        qT = qr.T
        gate = jnp.dot(kmean_ref[hh], qT, preferred_element_type=F32,
                       precision=lax.Precision.HIGHEST)
        gate = jnp.where(past, gate, -jnp.inf)
        selected = blk_id == q_blk
        for _ in range(MOBA_TOPK):
            best = jnp.max(gate, axis=0, keepdims=True)
            first = jnp.min(jnp.where(gate == best, blk_id, n_blk), axis=0, keepdims=True)
            pick = (blk_id == first) & (best > -jnp.inf)
            selected = selected | pick
            gate = jnp.where(pick, -jnp.inf, gate)
        qT_ref[hh, :HEAD_DIM, :] = (qT * (HEAD_DIM ** -0.5 * LOG2E)).astype(BF16)
        qT_ref[hh, HEAD_DIM:HEAD_DIM + n_blk, :] = jnp.where(selected, 0.0, MASK_BIAS).astype(BF16)
        qT_ref[hh, HEAD_DIM + n_blk:, :] = jnp.zeros((AUG_DIM - HEAD_DIM - n_blk, tm), BF16)

    pv = jnp.dot(h, w_ref[:, 2 * ATTN_WIDTH:3 * ATTN_WIDTH], preferred_element_type=F32)
    ones_row = (lax.broadcasted_iota(jnp.int32, (V_ROWS - HEAD_DIM, MOBA_BLOCK), 0) == 0).astype(BF16)
    for hh in range(N_ATTN_HEADS):
        vT = pv[:, hh * HEAD_DIM:(hh + 1) * HEAD_DIM].T.astype(BF16)
        for b in range(blocks_per_tile):
            vT_ref[hh, b, :HEAD_DIM, :] = vT[:, b * MOBA_BLOCK:(b + 1) * MOBA_BLOCK]
            vT_ref[hh, b, HEAD_DIM:, :] = ones_row

    c0 = 3 * ATTN_WIDTH
    pc = jnp.dot(h, w_ref[:, c0:c0 + 3 * CONV_CH], preferred_element_type=F32)
    u = pc[:, CONV_CH:2 * CONV_CH] * pc[:, 2 * CONV_CH:]
    prev1 = carry_ref[7:8, :]
    prev2 = carry_ref[6:7, :]
    row = lax.broadcasted_iota(jnp.int32, u.shape, 0)
    u1 = jnp.where(row == 0, prev1, pltpu.roll(u, 1, 0))
    u2 = jnp.where(row == 0, prev2, jnp.where(row == 1, prev1, pltpu.roll(u, 2, 0)))
    carry_ref[...] = u[tm - 8:, :]
    conv = u2 * cw_ref[0:1, :] + u1 * cw_ref[1:2, :] + u * cw_ref[2:3, :]
    ycm_ref[:, :CONV_CH] = (pc[:, :CONV_CH] * conv).astype(BF16)

    pm = jnp.dot(h, w_ref[:, c0 + 3 * CONV_CH:], preferred_element_type=F32)
    for hh in range(N_MEM_HEADS):
        sl = slice(hh * HEAD_DIM, (hh + 1) * HEAD_DIM)
        mq = (_rms(pm[:, sl], mqg_ref[...]) * HEAD_DIM ** -0.5).astype(BF16)
        sc = lax.dot_general(mq, mk_ref[:, sl], (((1,), (1,)), ((), ())), preferred_element_type=F32)
        p = jnp.exp(sc - jnp.max(sc, axis=-1, keepdims=True))
        o = jnp.dot(p.astype(BF16), mv_ref[:, sl], preferred_element_type=F32)
        o = o / jnp.sum(p, axis=-1, keepdims=True)
        ycm_ref[:, CONV_CH + hh * HEAD_DIM:CONV_CH + (hh + 1) * HEAD_DIM] = o.astype(BF16)


def _mix(x, gain, w_in, q_gain, k_gain, conv_w, cos, sin, mq_gain, mk, mv, layer, *, tm):
    s = x.shape[0]
    n_blk = s // MOBA_BLOCK
    bpt = tm // MOBA_BLOCK
    m = mk.shape[0]
    const = lambda i: (0, 0)
    return pl.pallas_call(
        functools.partial(_mix_kernel, tm=tm),
        grid=(s // tm,),
        in_specs=[pl.BlockSpec((tm, D_MODEL), lambda i: (i, 0)),
                  _resident((1, D_MODEL), const),
                  _resident((None, D_MODEL, IN_WIDTH), lambda i: (layer, 0, 0)),
                  _resident((1, HEAD_DIM), const),
                  _resident((1, HEAD_DIM), const),
                  _resident((CONV_K, CONV_CH), const),
                  pl.BlockSpec((tm, HEAD_DIM), lambda i: (i, 0)),
                  pl.BlockSpec((tm, HEAD_DIM), lambda i: (i, 0)),
                  _resident((1, HEAD_DIM), const),
                  _resident((m, MEM_WIDTH), const),
                  _resident((m, MEM_WIDTH), const)],
        out_specs=[pl.BlockSpec((N_ATTN_HEADS, AUG_DIM, tm), lambda i: (0, 0, i)),
                   pl.BlockSpec((N_ATTN_HEADS, bpt, MOBA_BLOCK, AUG_DIM), lambda i: (0, i, 0, 0)),
                   pl.BlockSpec((N_ATTN_HEADS, bpt, V_ROWS, MOBA_BLOCK), lambda i: (0, i, 0, 0)),
                   pl.BlockSpec((tm, CONV_CH + MEM_WIDTH), lambda i: (i, 0))],
        out_shape=[jax.ShapeDtypeStruct((N_ATTN_HEADS, AUG_DIM, s), BF16),
                   jax.ShapeDtypeStruct((N_ATTN_HEADS, n_blk, MOBA_BLOCK, AUG_DIM), BF16),
                   jax.ShapeDtypeStruct((N_ATTN_HEADS, n_blk, V_ROWS, MOBA_BLOCK), BF16),
                   jax.ShapeDtypeStruct((s, CONV_CH + MEM_WIDTH), BF16)],
        scratch_shapes=[pltpu.VMEM((N_ATTN_HEADS, n_blk, HEAD_DIM), F32),
                        pltpu.VMEM((8, CONV_CH), F32)],
        compiler_params=_cparams("arbitrary"),
        name="mix_proj",
    )(x, gain, w_in, q_gain, k_gain, conv_w, cos, sin, mq_gain, mk, mv)


Q_BLOCKS = 4
M_INIT = 2 * MASK_BIAS


def _attn_kernel(qT_ref, k_ref, vT_ref, o_ref, acc_ref, *, q_blocks):
    t = pl.program_id(1)
    qT = qT_ref[0]
    acc_ref[...] = jnp.zeros_like(acc_ref)

    def step(j, m, causal_block=None):
        s = jnp.dot(k_ref[0, j], qT, preferred_element_type=F32)
        if causal_block is not None:
            lo, hi = causal_block * MOBA_BLOCK, (causal_block + 1) * MOBA_BLOCK
            own = s[:, lo:hi]
            key_pos = lax.broadcasted_iota(jnp.int32, own.shape, 0)
            qry_pos = lax.broadcasted_iota(jnp.int32, own.shape, 1)
            own = jnp.where(key_pos <= qry_pos, own, MASK_BIAS)
            pieces = ([s[:, :lo]] if lo else []) + [own] + ([s[:, hi:]] if hi < s.shape[1] else [])
            s = jnp.concatenate(pieces, axis=1)
        m_new = jnp.maximum(m, jnp.max(s, axis=0, keepdims=True))
        p = jnp.exp2(s - m_new).astype(BF16)
        acc_ref[...] = (acc_ref[...] * jnp.exp2(m - m_new)
                        + jnp.dot(vT_ref[0, j], p, preferred_element_type=F32))
        return m_new

    def past_blocks(u, m):
        for d in range(q_blocks):
            m = step(u * q_blocks + d, m)
        return m

    m = jnp.full((1, q_blocks * MOBA_BLOCK), M_INIT, F32)
    m = lax.fori_loop(0, t, past_blocks, m)
    for d in range(q_blocks):
        m = step(t * q_blocks + d, m, causal_block=d)
    out = acc_ref[:HEAD_DIM, :] / acc_ref[HEAD_DIM:HEAD_DIM + 1, :]
    o_ref[...] = out.T.astype(o_ref.dtype)


def _attention(qT, k, vT):
    n_blk = k.shape[1]
    s = n_blk * MOBA_BLOCK
    q_blocks = math.gcd(Q_BLOCKS, n_blk)
    q_tile = q_blocks * MOBA_BLOCK
    return pl.pallas_call(
        functools.partial(_attn_kernel, q_blocks=q_blocks),
        grid=(N_ATTN_HEADS, s // q_tile),
        in_specs=[pl.BlockSpec((1, AUG_DIM, q_tile), lambda h, t: (h, 0, t)),
                  pl.BlockSpec((1, n_blk, MOBA_BLOCK, AUG_DIM), lambda h, t: (h, 0, 0, 0)),
                  pl.BlockSpec((1, n_blk, V_ROWS, MOBA_BLOCK), lambda h, t: (h, 0, 0, 0))],
        out_specs=pl.BlockSpec((q_tile, HEAD_DIM), lambda h, t: (t, h)),
        out_shape=jax.ShapeDtypeStruct((s, ATTN_WIDTH), BF16),
        scratch_shapes=[pltpu.VMEM((V_ROWS, q_tile), F32)],
        compiler_params=_cparams("parallel", "arbitrary"),
        name="moba_attn",
    )(qT, k, vT)


def _outproj_kernel(x_ref, ya_ref, ycm_ref, w_ref, o_ref):
    o_ref[...] = (x_ref[...]
                  + jnp.dot(ya_ref[...], w_ref[:ATTN_WIDTH], preferred_element_type=F32)
                  + jnp.dot(ycm_ref[...], w_ref[ATTN_WIDTH:], preferred_element_type=F32))


def _outproj(x, y_attn, y_cm, w_out, layer, *, tm):
    s = x.shape[0]
    return pl.pallas_call(
        _outproj_kernel,
        grid=(s // tm,),
        in_specs=[pl.BlockSpec((tm, D_MODEL), lambda i: (i, 0)),
                  pl.BlockSpec((tm, ATTN_WIDTH), lambda i: (i, 0)),
                  pl.BlockSpec((tm, CONV_CH + MEM_WIDTH), lambda i: (i, 0)),
                  _resident((None, D_MODEL, D_MODEL), lambda i: (layer, 0, 0))],
        out_specs=pl.BlockSpec((tm, D_MODEL), lambda i: (i, 0)),
        out_shape=jax.ShapeDtypeStruct((s, D_MODEL), F32),
        compiler_params=_cparams("parallel"),
        name="out_proj",
    )(x, y_attn, y_cm, w_out)


def kernel(x, mem, positions, ffn1_norm, ffn1_w_gate_up, ffn1_w_down, mix_norm, w_in, q_norm, k_norm,
           conv_w, mem_norm, w_mem_kv, mq_norm, mk_norm, w_out, ffn2_norm, ffn2_w_gate_up, ffn2_w_down):
    b, s, _ = x.shape
    assert b == 1 and s % MOBA_BLOCK == 0 and s // MOBA_BLOCK <= HEAD_DIM // 2
    depth = w_in.shape[0]
    tm = min(s, 512)
    ffn = functools.partial(_ffn, tm=tm, tf=512)

    w1gu, w1d, w2gu, w2d, w_in_b, w_out_b, w_mkv = (
        w.astype(BF16) for w in (ffn1_w_gate_up, ffn1_w_down, ffn2_w_gate_up, ffn2_w_down,
                                 w_in, w_out, w_mem_kv))
    cos, sin = _rope_tables(positions[0])
    xs = x[0]
    mem2 = mem[0]
    for l in range(depth):
        xs = ffn(xs, ffn1_norm[l][None], w1gu, w1d, l)
        mk, mv = _memkv(mem2, mem_norm[l][None], w_mkv, mk_norm[l][None], l)
        qT, k, vT, y_cm = _mix(xs, mix_norm[l][None], w_in_b, q_norm[l][None], k_norm[l][None],
                               conv_w[l], cos, sin, mq_norm[l][None], mk, mv, l, tm=tm)
        y_attn = _attention(qT, k, vT)
        xs = _outproj(xs, y_attn, y_cm, w_out_b, l, tm=tm)
        xs = ffn(xs, ffn2_norm[l][None], w2gu, w2d, l)
    return xs[None]
```

```python
import functools
import math

import jax
import jax.numpy as jnp
from jax import lax
from jax.experimental import pallas as pl
from jax.experimental.pallas import tpu as pltpu

F32 = jnp.float32
BF16 = jnp.bfloat16

D_MODEL = 2048
HEAD_DIM = 128
ATTN_WIDTH = 1024
N_ATTN_HEADS = 8
CONV_CH = 512
CONV_K = 3
MEM_WIDTH = 512
N_MEM_HEADS = 4
IN_WIDTH = 3 * ATTN_WIDTH + 3 * CONV_CH + MEM_WIDTH
D_FF = 5632
FFN_RES = 0.5
MOBA_BLOCK = 256
MOBA_TOPK = 3
ROPE_THETA = 10000.0
RMS_EPS = 1e-6

V7X_VMEM_BYTES = 64 * 1024 * 1024
VMEM_LIMIT = V7X_VMEM_BYTES - 8 * 1024 * 1024

MASK_BIAS = -1e30
AUG_DIM = 2 * HEAD_DIM
V_ROWS = HEAD_DIM + 16
LOG2E = math.log2(math.e)


def _rms(x, gain):
    return x * lax.rsqrt(jnp.mean(x * x, axis=-1, keepdims=True) + RMS_EPS) * gain


def _cparams(*sem):
    return pltpu.CompilerParams(dimension_semantics=sem, vmem_limit_bytes=VMEM_LIMIT)


def _resident(shape, index_map):
    return pl.BlockSpec(shape, index_map, pipeline_mode=pl.Buffered(1))


def _rope_kernel(pos_ref, inv_ref, cos_ref, sin_ref):
    ang = pos_ref[...].astype(F32) * inv_ref[...]
    lane = lax.broadcasted_iota(jnp.int32, ang.shape, 1)
    s = jnp.sin(ang)
    cos_ref[...] = jnp.cos(ang)
    sin_ref[...] = jnp.where(lane < HEAD_DIM // 2, -s, s)


def _rope_tables(positions):
    s = positions.shape[0]
    t = min(s, 2048)
    inv = ROPE_THETA ** (-jnp.arange(0, HEAD_DIM, 2, dtype=F32) / HEAD_DIM)
    inv = jnp.concatenate([inv, inv])[None, :]
    return pl.pallas_call(
        _rope_kernel,
        grid=(s // t,),
        in_specs=[pl.BlockSpec((t, 1), lambda i: (i, 0)),
                  pl.BlockSpec((1, HEAD_DIM), lambda i: (0, 0))],
        out_specs=[pl.BlockSpec((t, HEAD_DIM), lambda i: (i, 0))] * 2,
        out_shape=[jax.ShapeDtypeStruct((s, HEAD_DIM), F32)] * 2,
        compiler_params=_cparams("parallel"),
        name="rope_tables",
    )(positions[:, None], inv)


def _ffn_kernel(x_ref, g_ref, wg_ref, wu_ref, wd_ref, o_ref, h_ref):
    j = pl.program_id(1)

    @pl.when(j == 0)
    def _():
        x = x_ref[...]
        h_ref[...] = _rms(x, g_ref[...]).astype(BF16)
        o_ref[...] = x

    h = h_ref[...]
    gate = jnp.dot(h, wg_ref[...], preferred_element_type=F32)
    up = jnp.dot(h, wu_ref[...], preferred_element_type=F32)
    act = (gate * jax.nn.sigmoid(gate) * up * FFN_RES).astype(BF16)
    o_ref[...] += jnp.dot(act, wd_ref[...], preferred_element_type=F32)


def _ffn(x, gain, w_gate_up, w_down, layer, *, tm, tf):
    s = x.shape[0]
    n_f = D_FF // tf
    return pl.pallas_call(
        _ffn_kernel,
        grid=(s // tm, n_f),
        in_specs=[pl.BlockSpec((tm, D_MODEL), lambda i, j: (i, 0)),
                  pl.BlockSpec((1, D_MODEL), lambda i, j: (0, 0)),
                  pl.BlockSpec((None, D_MODEL, tf), lambda i, j: (layer, 0, j)),
                  pl.BlockSpec((None, D_MODEL, tf), lambda i, j: (layer, 0, j + n_f)),
                  pl.BlockSpec((None, tf, D_MODEL), lambda i, j: (layer, j, 0))],
        out_specs=pl.BlockSpec((tm, D_MODEL), lambda i, j: (i, 0)),
        out_shape=jax.ShapeDtypeStruct((s, D_MODEL), F32),
        scratch_shapes=[pltpu.VMEM((tm, D_MODEL), BF16)],
        compiler_params=_cparams("parallel", "arbitrary"),
        name="ffn",
    )(x, gain, w_gate_up, w_gate_up, w_down)


def _memkv_kernel(mem_ref, g_ref, w_ref, kg_ref, mk_ref, mv_ref):
    h = _rms(mem_ref[...], g_ref[...]).astype(BF16)
    kv = jnp.dot(h, w_ref[...], preferred_element_type=F32)
    for hh in range(N_MEM_HEADS):
        sl = slice(hh * HEAD_DIM, (hh + 1) * HEAD_DIM)
        mk_ref[:, sl] = _rms(kv[:, sl], kg_ref[...]).astype(BF16)
    mv_ref[...] = kv[:, MEM_WIDTH:].astype(BF16)


def _memkv(mem, gain, w_mem_kv, k_gain, layer):
    m = mem.shape[0]
    const = lambda i: (0, 0)
    return pl.pallas_call(
        _memkv_kernel,
        grid=(1,),
        in_specs=[pl.BlockSpec((m, D_MODEL), const),
                  pl.BlockSpec((1, D_MODEL), const),
                  pl.BlockSpec((None, D_MODEL, 2 * MEM_WIDTH), lambda i: (layer, 0, 0)),
                  pl.BlockSpec((1, HEAD_DIM), const)],
        out_specs=[pl.BlockSpec((m, MEM_WIDTH), const)] * 2,
        out_shape=[jax.ShapeDtypeStruct((m, MEM_WIDTH), BF16)] * 2,
        compiler_params=_cparams("arbitrary"),
        name="memkv",
    )(mem, gain, w_mem_kv, k_gain)


def _mix_kernel(x_ref, g_ref, w_ref, qg_ref, kg_ref, cw_ref, cos_ref, sin_ref, mqg_ref,
                mk_ref, mv_ref, qT_ref, k_ref, vT_ref, ycm_ref, kmean_ref, carry_ref, *, tm):
    i = pl.program_id(0)
    blocks_per_tile = tm // MOBA_BLOCK
    n_blk = kmean_ref.shape[1]

    @pl.when(i == 0)
    def _():
        kmean_ref[...] = jnp.zeros_like(kmean_ref)
        carry_ref[...] = jnp.zeros_like(carry_ref)

    h = _rms(x_ref[...], g_ref[...]).astype(BF16)
    cos = cos_ref[...]
    sin = sin_ref[...]

    def rope(t):
        return t * cos + pltpu.roll(t, HEAD_DIM // 2, 1) * sin

    pk = jnp.dot(h, w_ref[:, ATTN_WIDTH:2 * ATTN_WIDTH], preferred_element_type=F32)
    slot = lax.broadcasted_iota(jnp.int32, (tm, HEAD_DIM), 1)
    row_blk = (i * tm + lax.broadcasted_iota(jnp.int32, (tm, HEAD_DIM), 0)) // MOBA_BLOCK
    one_hot = (slot == row_blk).astype(BF16)
    for hh in range(N_ATTN_HEADS):
        kr = rope(_rms(pk[:, hh * HEAD_DIM:(hh + 1) * HEAD_DIM], kg_ref[...]))
        k_ref[hh, :, :HEAD_DIM] = kr.astype(BF16)
        k_ref[hh, :, HEAD_DIM:] = one_hot
        for b in range(blocks_per_tile):
            blk = kr[b * MOBA_BLOCK:(b + 1) * MOBA_BLOCK]
            kmean_ref[hh, pl.ds(i * blocks_per_tile + b, 1), :] = jnp.mean(blk, axis=0, keepdims=True)

    pq = jnp.dot(h, w_ref[:, :ATTN_WIDTH], preferred_element_type=F32)
    q_blk = (i * tm + lax.broadcasted_iota(jnp.int32, (1, tm), 1)) // MOBA_BLOCK
    blk_id = lax.broadcasted_iota(jnp.int32, (n_blk, tm), 0)
    past = blk_id < q_blk
    for hh in range(N_ATTN_HEADS):
        qr = rope(_rms(pq[:, hh * HEAD_DIM:(hh + 1) * HEAD_DIM], qg_ref[...]))
        qT = qr.T
        gate = jnp.dot(kmean_ref[hh], qT, preferred_element_type=F32,
                       precision=lax.Precision.HIGHEST)
        gate = jnp.where(past, gate, -jnp.inf)
        selected = blk_id == q_blk
        for _ in range(MOBA_TOPK):
            best = jnp.max(gate, axis=0, keepdims=True)
            first = jnp.min(jnp.where(gate == best, blk_id, n_blk), axis=0, keepdims=True)
            pick = (blk_id == first) & (best > -jnp.inf)
            selected = selected | pick
            gate = jnp.where(pick, -jnp.inf, gate)
        qT_ref[hh, :HEAD_DIM, :] = (qT * (HEAD_DIM ** -0.5 * LOG2E)).astype(BF16)
        qT_ref[hh, HEAD_DIM:HEAD_DIM + n_blk, :] = jnp.where(selected, 0.0, MASK_BIAS).astype(BF16)
        qT_ref[hh, HEAD_DIM + n_blk:, :] = jnp.zeros((AUG_DIM - HEAD_DIM - n_blk, tm), BF16)

    pv = jnp.dot(h, w_ref[:, 2 * ATTN_WIDTH:3 * ATTN_WIDTH], preferred_element_type=F32)
    ones_row = (lax.broadcasted_iota(jnp.int32, (V_ROWS - HEAD_DIM, tm), 0) == 0).astype(BF16)
    for hh in range(N_ATTN_HEADS):
        vT_ref[hh, 0, :HEAD_DIM, :] = pv[:, hh * HEAD_DIM:(hh + 1) * HEAD_DIM].T.astype(BF16)
        vT_ref[hh, 0, HEAD_DIM:, :] = ones_row

    c0 = 3 * ATTN_WIDTH
    pc = jnp.dot(h, w_ref[:, c0:c0 + 3 * CONV_CH], preferred_element_type=F32)
    u = pc[:, CONV_CH:2 * CONV_CH] * pc[:, 2 * CONV_CH:]
    prev1 = carry_ref[7:8, :]
    prev2 = carry_ref[6:7, :]
    row = lax.broadcasted_iota(jnp.int32, u.shape, 0)
    u1 = jnp.where(row == 0, prev1, pltpu.roll(u, 1, 0))
    u2 = jnp.where(row == 0, prev2, jnp.where(row == 1, prev1, pltpu.roll(u, 2, 0)))
    carry_ref[...] = u[tm - 8:, :]
    conv = u2 * cw_ref[0:1, :] + u1 * cw_ref[1:2, :] + u * cw_ref[2:3, :]
    ycm_ref[:, :CONV_CH] = (pc[:, :CONV_CH] * conv).astype(BF16)

    pm = jnp.dot(h, w_ref[:, c0 + 3 * CONV_CH:], preferred_element_type=F32)
    for hh in range(N_MEM_HEADS):
        sl = slice(hh * HEAD_DIM, (hh + 1) * HEAD_DIM)
        mq = (_rms(pm[:, sl], mqg_ref[...]) * HEAD_DIM ** -0.5).astype(BF16)
        sc = lax.dot_general(mq, mk_ref[:, sl], (((1,), (1,)), ((), ())), preferred_element_type=F32)
        p = jnp.exp(sc - jnp.max(sc, axis=-1, keepdims=True))
        o = jnp.dot(p.astype(BF16), mv_ref[:, sl], preferred_element_type=F32)
        o = o / jnp.sum(p, axis=-1, keepdims=True)
        ycm_ref[:, CONV_CH + hh * HEAD_DIM:CONV_CH + (hh + 1) * HEAD_DIM] = o.astype(BF16)


def _mix(x, gain, w_in, q_gain, k_gain, conv_w, cos, sin, mq_gain, mk, mv, layer, *, tm):
    s = x.shape[0]
    n_blk = s // MOBA_BLOCK
    bpt = tm // MOBA_BLOCK
    m = mk.shape[0]
    const = lambda i: (0, 0)
    return pl.pallas_call(
        functools.partial(_mix_kernel, tm=tm),
        grid=(s // tm,),
        in_specs=[pl.BlockSpec((tm, D_MODEL), lambda i: (i, 0)),
                  _resident((1, D_MODEL), const),
                  _resident((None, D_MODEL, IN_WIDTH), lambda i: (layer, 0, 0)),
                  _resident((1, HEAD_DIM), const),
                  _resident((1, HEAD_DIM), const),
                  _resident((CONV_K, CONV_CH), const),
                  pl.BlockSpec((tm, HEAD_DIM), lambda i: (i, 0)),
                  pl.BlockSpec((tm, HEAD_DIM), lambda i: (i, 0)),
                  _resident((1, HEAD_DIM), const),
                  _resident((m, MEM_WIDTH), const),
                  _resident((m, MEM_WIDTH), const)],
        out_specs=[pl.BlockSpec((N_ATTN_HEADS, AUG_DIM, tm), lambda i: (0, 0, i)),
                   pl.BlockSpec((N_ATTN_HEADS, tm, AUG_DIM), lambda i: (0, i, 0)),
                   pl.BlockSpec((N_ATTN_HEADS, 1, V_ROWS, tm), lambda i: (0, i, 0, 0)),
                   pl.BlockSpec((tm, CONV_CH + MEM_WIDTH), lambda i: (i, 0))],
        out_shape=[jax.ShapeDtypeStruct((N_ATTN_HEADS, AUG_DIM, s), BF16),
                   jax.ShapeDtypeStruct((N_ATTN_HEADS, s, AUG_DIM), BF16),
                   jax.ShapeDtypeStruct((N_ATTN_HEADS, s // tm, V_ROWS, tm), BF16),
                   jax.ShapeDtypeStruct((s, CONV_CH + MEM_WIDTH), BF16)],
        scratch_shapes=[pltpu.VMEM((N_ATTN_HEADS, n_blk, HEAD_DIM), F32),
                        pltpu.VMEM((8, CONV_CH), F32)],
        compiler_params=_cparams("arbitrary"),
        name="mix_proj",
    )(x, gain, w_in, q_gain, k_gain, conv_w, cos, sin, mq_gain, mk, mv)


Q_TILE = 1024
KEY_TILE = 512
M_INIT = 2 * MASK_BIAS


def _causal_fix(s, first_lane_block):
    rows = []
    for b in range(s.shape[0] // MOBA_BLOCK):
        sb = s[b * MOBA_BLOCK:(b + 1) * MOBA_BLOCK]
        lo = (first_lane_block + b) * MOBA_BLOCK
        hi = lo + MOBA_BLOCK
        own = sb[:, lo:hi]
        key_pos = lax.broadcasted_iota(jnp.int32, own.shape, 0)
        qry_pos = lax.broadcasted_iota(jnp.int32, own.shape, 1)
        own = jnp.where(key_pos <= qry_pos, own, MASK_BIAS)
        pieces = ([sb[:, :lo]] if lo else []) + [own] + ([sb[:, hi:]] if hi < s.shape[1] else [])
        rows.append(jnp.concatenate(pieces, axis=1))
    return jnp.concatenate(rows, axis=0)


def _attn_kernel(qT_ref, k_ref, vT_ref, o_ref, acc_ref, *, q_tile, key_tile):
    t = pl.program_id(1)
    qT = qT_ref[0]
    acc_ref[...] = jnp.zeros_like(acc_ref)
    steps_per_tile = q_tile // key_tile

    def scores(g, lane0, diagonal):
        rows = pl.ds(pl.multiple_of(g * key_tile, key_tile), key_tile)
        s = jnp.dot(k_ref[0, rows, :], qT[:, lane0:], preferred_element_type=F32)
        return _causal_fix(s, 0) if diagonal else s

    def update(g, s, m, lane0):
        m_old = m[:, lane0:]
        m_new = jnp.maximum(m_old, jnp.max(s, axis=0, keepdims=True))
        p = jnp.exp2(s - m_new).astype(BF16)
        acc_ref[:, lane0:] = (acc_ref[:, lane0:] * jnp.exp2(m_old - m_new)
                              + jnp.dot(vT_ref[0, g], p, preferred_element_type=F32))
        return jnp.concatenate([m[:, :lane0], m_new], axis=1) if lane0 else m_new

    def key_tiles(first, m, diagonal):
        tiles = [(first + d, d * key_tile if diagonal else 0) for d in range(steps_per_tile)]
        ss = [scores(g, lane0, diagonal) for g, lane0 in tiles]
        for (g, lane0), s in zip(tiles, ss):
            m = update(g, s, m, lane0)
        return m

    m = jnp.full((1, q_tile), M_INIT, F32)
    m = lax.fori_loop(0, t, lambda u, m: key_tiles(u * steps_per_tile, m, False), m)
    m = key_tiles(t * steps_per_tile, m, True)
    out = acc_ref[:HEAD_DIM, :] / acc_ref[HEAD_DIM:HEAD_DIM + 1, :]
    o_ref[...] = out.T.astype(o_ref.dtype)


def _attention(qT, k, vT):
    s = k.shape[1]
    key_tile = vT.shape[3]
    q_tile = min(Q_TILE, s)
    return pl.pallas_call(
        functools.partial(_attn_kernel, q_tile=q_tile, key_tile=key_tile),
        grid=(N_ATTN_HEADS, s // q_tile),
        in_specs=[pl.BlockSpec((1, AUG_DIM, q_tile), lambda h, t: (h, 0, t)),
                  pl.BlockSpec((1, s, AUG_DIM), lambda h, t: (h, 0, 0)),
                  pl.BlockSpec((1, s // key_tile, V_ROWS, key_tile), lambda h, t: (h, 0, 0, 0))],
        out_specs=pl.BlockSpec((q_tile, HEAD_DIM), lambda h, t: (t, h)),
        out_shape=jax.ShapeDtypeStruct((s, ATTN_WIDTH), BF16),
        scratch_shapes=[pltpu.VMEM((V_ROWS, q_tile), F32)],
        compiler_params=_cparams("parallel", "arbitrary"),
        name="moba_attn",
    )(qT, k, vT)


def _outproj_kernel(x_ref, ya_ref, ycm_ref, w_ref, o_ref):
    o_ref[...] = (x_ref[...]
                  + jnp.dot(ya_ref[...], w_ref[:ATTN_WIDTH], preferred_element_type=F32)
                  + jnp.dot(ycm_ref[...], w_ref[ATTN_WIDTH:], preferred_element_type=F32))


def _outproj(x, y_attn, y_cm, w_out, layer, *, tm):
    s = x.shape[0]
    return pl.pallas_call(
        _outproj_kernel,
        grid=(s // tm,),
        in_specs=[pl.BlockSpec((tm, D_MODEL), lambda i: (i, 0)),
                  pl.BlockSpec((tm, ATTN_WIDTH), lambda i: (i, 0)),
                  pl.BlockSpec((tm, CONV_CH + MEM_WIDTH), lambda i: (i, 0)),
                  _resident((None, D_MODEL, D_MODEL), lambda i: (layer, 0, 0))],
        out_specs=pl.BlockSpec((tm, D_MODEL), lambda i: (i, 0)),
        out_shape=jax.ShapeDtypeStruct((s, D_MODEL), F32),
        compiler_params=_cparams("parallel"),
        name="out_proj",
    )(x, y_attn, y_cm, w_out)


def kernel(x, mem, positions, ffn1_norm, ffn1_w_gate_up, ffn1_w_down, mix_norm, w_in, q_norm, k_norm,
           conv_w, mem_norm, w_mem_kv, mq_norm, mk_norm, w_out, ffn2_norm, ffn2_w_gate_up, ffn2_w_down):
    b, s, _ = x.shape
    assert b == 1 and s % MOBA_BLOCK == 0 and s // MOBA_BLOCK <= HEAD_DIM // 2
    depth = w_in.shape[0]
    tm = min(s, KEY_TILE)
    ffn = functools.partial(_ffn, tm=tm, tf=512)

    w1gu, w1d, w2gu, w2d, w_in_b, w_out_b, w_mkv = (
        w.astype(BF16) for w in (ffn1_w_gate_up, ffn1_w_down, ffn2_w_gate_up, ffn2_w_down,
                                 w_in, w_out, w_mem_kv))
    cos, sin = _rope_tables(positions[0])
    xs = x[0]
    mem2 = mem[0]
    for l in range(depth):
        xs = ffn(xs, ffn1_norm[l][None], w1gu, w1d, l)
        mk, mv = _memkv(mem2, mem_norm[l][None], w_mkv, mk_norm[l][None], l)
        qT, k, vT, y_cm = _mix(xs, mix_norm[l][None], w_in_b, q_norm[l][None], k_norm[l][None],
                               conv_w[l], cos, sin, mq_norm[l][None], mk, mv, l, tm=tm)
        y_attn = _attention(qT, k, vT)
        xs = _outproj(xs, y_attn, y_cm, w_out_b, l, tm=tm)
        xs = ffn(xs, ffn2_norm[l][None], w2gu, w2d, l)
    return xs[None]
```

```python
import functools
import math

import jax
import jax.numpy as jnp
from jax import lax
from jax.experimental import pallas as pl
from jax.experimental.pallas import tpu as pltpu

F32 = jnp.float32
BF16 = jnp.bfloat16

D_MODEL = 2048
HEAD_DIM = 128
ATTN_WIDTH = 1024
N_ATTN_HEADS = 8
CONV_CH = 512
CONV_K = 3
MEM_WIDTH = 512
N_MEM_HEADS = 4
IN_WIDTH = 3 * ATTN_WIDTH + 3 * CONV_CH + MEM_WIDTH
D_FF = 5632
FFN_RES = 0.5
MOBA_BLOCK = 256
MOBA_TOPK = 3
ROPE_THETA = 10000.0
RMS_EPS = 1e-6

V7X_VMEM_BYTES = 64 * 1024 * 1024
VMEM_LIMIT = V7X_VMEM_BYTES - 8 * 1024 * 1024

MASK_BIAS = -1e30
AUG_DIM = 2 * HEAD_DIM
V_ROWS = HEAD_DIM + 16
LOG2E = math.log2(math.e)


def _rms(x, gain):
    return x * lax.rsqrt(jnp.mean(x * x, axis=-1, keepdims=True) + RMS_EPS) * gain


def _cparams(*sem):
    return pltpu.CompilerParams(dimension_semantics=sem, vmem_limit_bytes=VMEM_LIMIT)


def _resident(shape, index_map):
    return pl.BlockSpec(shape, index_map, pipeline_mode=pl.Buffered(1))


def _rope_kernel(pos_ref, inv_ref, cos_ref, sin_ref):
    ang = pos_ref[...].astype(F32) * inv_ref[...]
    lane = lax.broadcasted_iota(jnp.int32, ang.shape, 1)
    s = jnp.sin(ang)
    cos_ref[...] = jnp.cos(ang)
    sin_ref[...] = jnp.where(lane < HEAD_DIM // 2, -s, s)


def _rope_tables(positions):
    s = positions.shape[0]
    t = min(s, 2048)
    inv = ROPE_THETA ** (-jnp.arange(0, HEAD_DIM, 2, dtype=F32) / HEAD_DIM)
    inv = jnp.concatenate([inv, inv])[None, :]
    return pl.pallas_call(
        _rope_kernel,
        grid=(s // t,),
        in_specs=[pl.BlockSpec((t, 1), lambda i: (i, 0)),
                  pl.BlockSpec((1, HEAD_DIM), lambda i: (0, 0))],
        out_specs=[pl.BlockSpec((t, HEAD_DIM), lambda i: (i, 0))] * 2,
        out_shape=[jax.ShapeDtypeStruct((s, HEAD_DIM), F32)] * 2,
        compiler_params=_cparams("parallel"),
        name="rope_tables",
    )(positions[:, None], inv)


def _ffn_kernel(x_ref, g_ref, wg_ref, wu_ref, wd_ref, o_ref, h_ref):
    j = pl.program_id(1)

    @pl.when(j == 0)
    def _():
        x = x_ref[...]
        h_ref[...] = _rms(x, g_ref[...]).astype(BF16)
        o_ref[...] = x

    h = h_ref[...]
    gate = jnp.dot(h, wg_ref[...], preferred_element_type=F32)
    up = jnp.dot(h, wu_ref[...], preferred_element_type=F32)
    act = (gate * jax.nn.sigmoid(gate) * up * FFN_RES).astype(BF16)
    o_ref[...] += jnp.dot(act, wd_ref[...], preferred_element_type=F32)


def _ffn(x, gain, w_gate_up, w_down, layer, *, tm, tf):
    s = x.shape[0]
    n_f = D_FF // tf
    return pl.pallas_call(
        _ffn_kernel,
        grid=(s // tm, n_f),
        in_specs=[pl.BlockSpec((tm, D_MODEL), lambda i, j: (i, 0)),
                  pl.BlockSpec((1, D_MODEL), lambda i, j: (0, 0)),
                  pl.BlockSpec((None, D_MODEL, tf), lambda i, j: (layer, 0, j)),
                  pl.BlockSpec((None, D_MODEL, tf), lambda i, j: (layer, 0, j + n_f)),
                  pl.BlockSpec((None, tf, D_MODEL), lambda i, j: (layer, j, 0))],
        out_specs=pl.BlockSpec((tm, D_MODEL), lambda i, j: (i, 0)),
        out_shape=jax.ShapeDtypeStruct((s, D_MODEL), F32),
        scratch_shapes=[pltpu.VMEM((tm, D_MODEL), BF16)],
        compiler_params=_cparams("parallel", "arbitrary"),
        name="ffn",
    )(x, gain, w_gate_up, w_gate_up, w_down)


def _memkv_kernel(mem_ref, g_ref, w_ref, kg_ref, mk_ref, mv_ref):
    h = _rms(mem_ref[...], g_ref[...]).astype(BF16)
    kv = jnp.dot(h, w_ref[...], preferred_element_type=F32)
    for hh in range(N_MEM_HEADS):
        sl = slice(hh * HEAD_DIM, (hh + 1) * HEAD_DIM)
        mk_ref[:, sl] = _rms(kv[:, sl], kg_ref[...]).astype(BF16)
    mv_ref[...] = kv[:, MEM_WIDTH:].astype(BF16)


def _memkv(mem, gain, w_mem_kv, k_gain, layer):
    m = mem.shape[0]
    const = lambda i: (0, 0)
    return pl.pallas_call(
        _memkv_kernel,
        grid=(1,),
        in_specs=[pl.BlockSpec((m, D_MODEL), const),
                  pl.BlockSpec((1, D_MODEL), const),
                  pl.BlockSpec((None, D_MODEL, 2 * MEM_WIDTH), lambda i: (layer, 0, 0)),
                  pl.BlockSpec((1, HEAD_DIM), const)],
        out_specs=[pl.BlockSpec((m, MEM_WIDTH), const)] * 2,
        out_shape=[jax.ShapeDtypeStruct((m, MEM_WIDTH), BF16)] * 2,
        compiler_params=_cparams("arbitrary"),
        name="memkv",
    )(mem, gain, w_mem_kv, k_gain)


def _mix_kernel(x_ref, g_ref, w_ref, qg_ref, kg_ref, cw_ref, cos_ref, sin_ref, mqg_ref,
                mk_ref, mv_ref, qT_ref, k_ref, vT_ref, ycm_ref, kmean_ref, carry_ref, *, tm):
    i = pl.program_id(0)
    blocks_per_tile = tm // MOBA_BLOCK
    n_blk = kmean_ref.shape[1]

    @pl.when(i == 0)
    def _():
        kmean_ref[...] = jnp.zeros_like(kmean_ref)
        carry_ref[...] = jnp.zeros_like(carry_ref)

    h = _rms(x_ref[...], g_ref[...]).astype(BF16)
    cos = cos_ref[...]
    sin = sin_ref[...]

    def rope(t):
        return t * cos + pltpu.roll(t, HEAD_DIM // 2, 1) * sin

    c0 = 3 * ATTN_WIDTH
    pk = jnp.dot(h, w_ref[:, ATTN_WIDTH:2 * ATTN_WIDTH], preferred_element_type=F32)
    pq = jnp.dot(h, w_ref[:, :ATTN_WIDTH], preferred_element_type=F32)
    pv = jnp.dot(h, w_ref[:, 2 * ATTN_WIDTH:c0], preferred_element_type=F32)
    pc = jnp.dot(h, w_ref[:, c0:c0 + 3 * CONV_CH], preferred_element_type=F32)
    pm = jnp.dot(h, w_ref[:, c0 + 3 * CONV_CH:], preferred_element_type=F32)
    slot = lax.broadcasted_iota(jnp.int32, (tm, HEAD_DIM), 1)
    row_blk = (i * tm + lax.broadcasted_iota(jnp.int32, (tm, HEAD_DIM), 0)) // MOBA_BLOCK
    one_hot = (slot == row_blk).astype(BF16)
    for hh in range(N_ATTN_HEADS):
        kr = rope(_rms(pk[:, hh * HEAD_DIM:(hh + 1) * HEAD_DIM], kg_ref[...]))
        k_ref[hh, :, :HEAD_DIM] = kr.astype(BF16)
        k_ref[hh, :, HEAD_DIM:] = one_hot
        for b in range(blocks_per_tile):
            blk = kr[b * MOBA_BLOCK:(b + 1) * MOBA_BLOCK]
            kmean_ref[hh, pl.ds(i * blocks_per_tile + b, 1), :] = jnp.mean(blk, axis=0, keepdims=True)

    q_blk = (i * tm + lax.broadcasted_iota(jnp.int32, (1, tm), 1)) // MOBA_BLOCK
    blk_id = lax.broadcasted_iota(jnp.int32, (n_blk, tm), 0)
    past = blk_id < q_blk
    for hh in range(N_ATTN_HEADS):
        qr = rope(_rms(pq[:, hh * HEAD_DIM:(hh + 1) * HEAD_DIM], qg_ref[...]))
        qT = qr.T
        gate = jnp.dot(kmean_ref[hh], qT, preferred_element_type=F32,
                       precision=lax.Precision.HIGHEST)
        gate = jnp.where(past, gate, -jnp.inf)
        selected = blk_id == q_blk
        for _ in range(MOBA_TOPK):
            best = jnp.max(gate, axis=0, keepdims=True)
            first = jnp.min(jnp.where(gate == best, blk_id, n_blk), axis=0, keepdims=True)
            pick = (blk_id == first) & (best > -jnp.inf)
            selected = selected | pick
            gate = jnp.where(pick, -jnp.inf, gate)
        qT_ref[hh, :HEAD_DIM, :] = (qT * (HEAD_DIM ** -0.5 * LOG2E)).astype(BF16)
        qT_ref[hh, HEAD_DIM:HEAD_DIM + n_blk, :] = jnp.where(selected, 0.0, MASK_BIAS).astype(BF16)
        qT_ref[hh, HEAD_DIM + n_blk:, :] = jnp.zeros((AUG_DIM - HEAD_DIM - n_blk, tm), BF16)

    ones_row = (lax.broadcasted_iota(jnp.int32, (V_ROWS - HEAD_DIM, tm), 0) == 0).astype(BF16)
    for hh in range(N_ATTN_HEADS):
        vT_ref[hh, 0, :HEAD_DIM, :] = pv[:, hh * HEAD_DIM:(hh + 1) * HEAD_DIM].T.astype(BF16)
        vT_ref[hh, 0, HEAD_DIM:, :] = ones_row

    u = pc[:, CONV_CH:2 * CONV_CH] * pc[:, 2 * CONV_CH:]
    prev1 = carry_ref[7:8, :]
    prev2 = carry_ref[6:7, :]
    row = lax.broadcasted_iota(jnp.int32, u.shape, 0)
    u1 = jnp.where(row == 0, prev1, pltpu.roll(u, 1, 0))
    u2 = jnp.where(row == 0, prev2, jnp.where(row == 1, prev1, pltpu.roll(u, 2, 0)))
    carry_ref[...] = u[tm - 8:, :]
    conv = u2 * cw_ref[0:1, :] + u1 * cw_ref[1:2, :] + u * cw_ref[2:3, :]
    ycm_ref[:, :CONV_CH] = (pc[:, :CONV_CH] * conv).astype(BF16)

    for hh in range(N_MEM_HEADS):
        sl = slice(hh * HEAD_DIM, (hh + 1) * HEAD_DIM)
        mq = (_rms(pm[:, sl], mqg_ref[...]) * HEAD_DIM ** -0.5).astype(BF16)
        sc = lax.dot_general(mq, mk_ref[:, sl], (((1,), (1,)), ((), ())), preferred_element_type=F32)
        p = jnp.exp(sc - jnp.max(sc, axis=-1, keepdims=True))
        o = jnp.dot(p.astype(BF16), mv_ref[:, sl], preferred_element_type=F32)
        o = o / jnp.sum(p, axis=-1, keepdims=True)
        ycm_ref[:, CONV_CH + hh * HEAD_DIM:CONV_CH + (hh + 1) * HEAD_DIM] = o.astype(BF16)


def _mix(x, gain, w_in, q_gain, k_gain, conv_w, cos, sin, mq_gain, mk, mv, layer, *, tm):
    s = x.shape[0]
    n_blk = s // MOBA_BLOCK
    bpt = tm // MOBA_BLOCK
    m = mk.shape[0]
    const = lambda i: (0, 0)
    return pl.pallas_call(
        functools.partial(_mix_kernel, tm=tm),
        grid=(s // tm,),
        in_specs=[pl.BlockSpec((tm, D_MODEL), lambda i: (i, 0)),
                  _resident((1, D_MODEL), const),
                  _resident((None, D_MODEL, IN_WIDTH), lambda i: (layer, 0, 0)),
                  _resident((1, HEAD_DIM), const),
                  _resident((1, HEAD_DIM), const),
                  _resident((CONV_K, CONV_CH), const),
                  pl.BlockSpec((tm, HEAD_DIM), lambda i: (i, 0)),
                  pl.BlockSpec((tm, HEAD_DIM), lambda i: (i, 0)),
                  _resident((1, HEAD_DIM), const),
                  _resident((m, MEM_WIDTH), const),
                  _resident((m, MEM_WIDTH), const)],
        out_specs=[pl.BlockSpec((N_ATTN_HEADS, AUG_DIM, tm), lambda i: (0, 0, i)),
                   pl.BlockSpec((N_ATTN_HEADS, tm, AUG_DIM), lambda i: (0, i, 0)),
                   pl.BlockSpec((N_ATTN_HEADS, 1, V_ROWS, tm), lambda i: (0, i, 0, 0)),
                   pl.BlockSpec((tm, CONV_CH + MEM_WIDTH), lambda i: (i, 0))],
        out_shape=[jax.ShapeDtypeStruct((N_ATTN_HEADS, AUG_DIM, s), BF16),
                   jax.ShapeDtypeStruct((N_ATTN_HEADS, s, AUG_DIM), BF16),
                   jax.ShapeDtypeStruct((N_ATTN_HEADS, s // tm, V_ROWS, tm), BF16),
                   jax.ShapeDtypeStruct((s, CONV_CH + MEM_WIDTH), BF16)],
        scratch_shapes=[pltpu.VMEM((N_ATTN_HEADS, n_blk, HEAD_DIM), F32),
                        pltpu.VMEM((8, CONV_CH), F32)],
        compiler_params=_cparams("arbitrary"),
        name="mix_proj",
    )(x, gain, w_in, q_gain, k_gain, conv_w, cos, sin, mq_gain, mk, mv)


Q_TILE = 2048
KEY_TILE = 512
M_INIT = 2 * MASK_BIAS


def _causal_fix(s, first_lane_block):
    rows = []
    for b in range(s.shape[0] // MOBA_BLOCK):
        sb = s[b * MOBA_BLOCK:(b + 1) * MOBA_BLOCK]
        lo = (first_lane_block + b) * MOBA_BLOCK
        hi = lo + MOBA_BLOCK
        own = sb[:, lo:hi]
        key_pos = lax.broadcasted_iota(jnp.int32, own.shape, 0)
        qry_pos = lax.broadcasted_iota(jnp.int32, own.shape, 1)
        own = jnp.where(key_pos <= qry_pos, own, MASK_BIAS)
        pieces = ([sb[:, :lo]] if lo else []) + [own] + ([sb[:, hi:]] if hi < s.shape[1] else [])
        rows.append(jnp.concatenate(pieces, axis=1))
    return jnp.concatenate(rows, axis=0)


def _attn_kernel(qT_ref, k_ref, vT_ref, o_ref, acc_ref, *, q_tile, key_tile):
    t = pl.program_id(1)
    qT = qT_ref[0]
    acc_ref[...] = jnp.zeros_like(acc_ref)
    steps_per_tile = q_tile // key_tile

    def scores(g, lane0, diagonal):
        rows = pl.ds(pl.multiple_of(g * key_tile, key_tile), key_tile)
        s = jnp.dot(k_ref[0, rows, :], qT[:, lane0:], preferred_element_type=F32)
        return _causal_fix(s, 0) if diagonal else s

    def update(g, s, m, lane0):
        m_old = m[:, lane0:]
        m_new = jnp.maximum(m_old, jnp.max(s, axis=0, keepdims=True))
        p = jnp.exp2(s - m_new).astype(BF16)
        acc_ref[:, lane0:] = (acc_ref[:, lane0:] * jnp.exp2(m_old - m_new)
                              + jnp.dot(vT_ref[0, g], p, preferred_element_type=F32))
        return jnp.concatenate([m[:, :lane0], m_new], axis=1) if lane0 else m_new

    def key_tiles(first, m, diagonal):
        tiles = [(first + d, d * key_tile if diagonal else 0) for d in range(steps_per_tile)]
        ss = [scores(g, lane0, diagonal) for g, lane0 in tiles]
        for (g, lane0), s in zip(tiles, ss):
            m = update(g, s, m, lane0)
        return m

    m = jnp.full((1, q_tile), M_INIT, F32)
    m = lax.fori_loop(0, t, lambda u, m: key_tiles(u * steps_per_tile, m, False), m)
    m = key_tiles(t * steps_per_tile, m, True)
    out = acc_ref[:HEAD_DIM, :] / acc_ref[HEAD_DIM:HEAD_DIM + 1, :]
    o_ref[...] = out.T.astype(o_ref.dtype)


def _attention(qT, k, vT):
    s = k.shape[1]
    key_tile = vT.shape[3]
    q_tile = min(Q_TILE, s)
    return pl.pallas_call(
        functools.partial(_attn_kernel, q_tile=q_tile, key_tile=key_tile),
        grid=(N_ATTN_HEADS, s // q_tile),
        in_specs=[pl.BlockSpec((1, AUG_DIM, q_tile), lambda h, t: (h, 0, t)),
                  _resident((1, s, AUG_DIM), lambda h, t: (h, 0, 0)),
                  _resident((1, s // key_tile, V_ROWS, key_tile), lambda h, t: (h, 0, 0, 0))],
        out_specs=pl.BlockSpec((q_tile, HEAD_DIM), lambda h, t: (t, h)),
        out_shape=jax.ShapeDtypeStruct((s, ATTN_WIDTH), BF16),
        scratch_shapes=[pltpu.VMEM((V_ROWS, q_tile), F32)],
        compiler_params=_cparams("parallel", "arbitrary"),
        name="moba_attn",
    )(qT, k, vT)


def _outproj_kernel(x_ref, ya_ref, ycm_ref, w_ref, o_ref):
    o_ref[...] = (x_ref[...]
                  + jnp.dot(ya_ref[...], w_ref[:ATTN_WIDTH], preferred_element_type=F32)
                  + jnp.dot(ycm_ref[...], w_ref[ATTN_WIDTH:], preferred_element_type=F32))


def _outproj(x, y_attn, y_cm, w_out, layer, *, tm):
    s = x.shape[0]
    return pl.pallas_call(
        _outproj_kernel,
        grid=(s // tm,),
        in_specs=[pl.BlockSpec((tm, D_MODEL), lambda i: (i, 0)),
                  pl.BlockSpec((tm, ATTN_WIDTH), lambda i: (i, 0)),
                  pl.BlockSpec((tm, CONV_CH + MEM_WIDTH), lambda i: (i, 0)),
                  _resident((None, D_MODEL, D_MODEL), lambda i: (layer, 0, 0))],
        out_specs=pl.BlockSpec((tm, D_MODEL), lambda i: (i, 0)),
        out_shape=jax.ShapeDtypeStruct((s, D_MODEL), F32),
        compiler_params=_cparams("parallel"),
        name="out_proj",
    )(x, y_attn, y_cm, w_out)


def kernel(x, mem, positions, ffn1_norm, ffn1_w_gate_up, ffn1_w_down, mix_norm, w_in, q_norm, k_norm,
           conv_w, mem_norm, w_mem_kv, mq_norm, mk_norm, w_out, ffn2_norm, ffn2_w_gate_up, ffn2_w_down):
    b, s, _ = x.shape
    assert b == 1 and s % MOBA_BLOCK == 0 and s // MOBA_BLOCK <= HEAD_DIM // 2
    depth = w_in.shape[0]
    tm = min(s, KEY_TILE)
    ffn = functools.partial(_ffn, tm=tm, tf=512)

    w1gu, w1d, w2gu, w2d, w_in_b, w_out_b, w_mkv = (
        w.astype(BF16) for w in (ffn1_w_gate_up, ffn1_w_down, ffn2_w_gate_up, ffn2_w_down,
                                 w_in, w_out, w_mem_kv))
    cos, sin = _rope_tables(positions[0])
    xs = x[0]
    mem2 = mem[0]
    for l in range(depth):
        xs = ffn(xs, ffn1_norm[l][None], w1gu, w1d, l)
        mk, mv = _memkv(mem2, mem_norm[l][None], w_mkv, mk_norm[l][None], l)
        qT, k, vT, y_cm = _mix(xs, mix_norm[l][None], w_in_b, q_norm[l][None], k_norm[l][None],
                               conv_w[l], cos, sin, mq_norm[l][None], mk, mv, l, tm=tm)
        y_attn = _attention(qT, k, vT)
        xs = _outproj(xs, y_attn, y_cm, w_out_b, l, tm=tm)
        xs = ffn(xs, ffn2_norm[l][None], w2gu, w2d, l)
    return xs[None]
```

```python
import functools
import math

import jax
import jax.numpy as jnp
from jax import lax
from jax.experimental import pallas as pl
from jax.experimental.pallas import tpu as pltpu

F32 = jnp.float32
BF16 = jnp.bfloat16

D_MODEL = 2048
HEAD_DIM = 128
ATTN_WIDTH = 1024
N_ATTN_HEADS = 8
CONV_CH = 512
CONV_K = 3
MEM_WIDTH = 512
N_MEM_HEADS = 4
IN_WIDTH = 3 * ATTN_WIDTH + 3 * CONV_CH + MEM_WIDTH
D_FF = 5632
FFN_RES = 0.5
MOBA_BLOCK = 256
MOBA_TOPK = 3
ROPE_THETA = 10000.0
RMS_EPS = 1e-6

V7X_VMEM_BYTES = 64 * 1024 * 1024
VMEM_LIMIT = V7X_VMEM_BYTES - 8 * 1024 * 1024

MASK_BIAS = -1e30
AUG_DIM = 2 * HEAD_DIM
V_ROWS = HEAD_DIM + 16
LOG2E = math.log2(math.e)


def _rms(x, gain):
    return x * lax.rsqrt(jnp.mean(x * x, axis=-1, keepdims=True) + RMS_EPS) * gain


def _cparams(*sem):
    return pltpu.CompilerParams(dimension_semantics=sem, vmem_limit_bytes=VMEM_LIMIT)


def _resident(shape, index_map):
    return pl.BlockSpec(shape, index_map, pipeline_mode=pl.Buffered(1))


def _rope_kernel(pos_ref, inv_ref, cos_ref, sin_ref):
    ang = pos_ref[...].astype(F32) * inv_ref[...]
    lane = lax.broadcasted_iota(jnp.int32, ang.shape, 1)
    s = jnp.sin(ang)
    cos_ref[...] = jnp.cos(ang)
    sin_ref[...] = jnp.where(lane < HEAD_DIM // 2, -s, s)


def _rope_tables(positions):
    s = positions.shape[0]
    t = min(s, 2048)
    inv = ROPE_THETA ** (-jnp.arange(0, HEAD_DIM, 2, dtype=F32) / HEAD_DIM)
    inv = jnp.concatenate([inv, inv])[None, :]
    return pl.pallas_call(
        _rope_kernel,
        grid=(s // t,),
        in_specs=[pl.BlockSpec((t, 1), lambda i: (i, 0)),
                  pl.BlockSpec((1, HEAD_DIM), lambda i: (0, 0))],
        out_specs=[pl.BlockSpec((t, HEAD_DIM), lambda i: (i, 0))] * 2,
        out_shape=[jax.ShapeDtypeStruct((s, HEAD_DIM), F32)] * 2,
        compiler_params=_cparams("parallel"),
        name="rope_tables",
    )(positions[:, None], inv)


FFN_ROW_TILE = 1024


def _ffn_kernel(x_ref, g_ref, wg_ref, wu_ref, wd_ref, o_ref, h_ref):
    j = pl.program_id(1)

    @pl.when(j == 0)
    def _():
        x = x_ref[...]
        h_ref[...] = _rms(x, g_ref[...]).astype(BF16)
        o_ref[...] = x

    h = h_ref[...]
    gate = jnp.dot(h, wg_ref[...], preferred_element_type=F32)
    up = jnp.dot(h, wu_ref[...], preferred_element_type=F32)
    act = (gate * jax.nn.sigmoid(gate) * up * FFN_RES).astype(BF16)
    o_ref[...] += jnp.dot(act, wd_ref[...], preferred_element_type=F32)


def _ffn(x, gain, w_gate_up, w_down, layer, *, tm, tf):
    s = x.shape[0]
    n_f = D_FF // tf
    return pl.pallas_call(
        _ffn_kernel,
        grid=(s // tm, n_f),
        in_specs=[pl.BlockSpec((tm, D_MODEL), lambda i, j: (i, 0)),
                  pl.BlockSpec((1, D_MODEL), lambda i, j: (0, 0)),
                  pl.BlockSpec((None, D_MODEL, tf), lambda i, j: (layer, 0, j)),
                  pl.BlockSpec((None, D_MODEL, tf), lambda i, j: (layer, 0, j + n_f)),
                  pl.BlockSpec((None, tf, D_MODEL), lambda i, j: (layer, j, 0))],
        out_specs=pl.BlockSpec((tm, D_MODEL), lambda i, j: (i, 0)),
        out_shape=jax.ShapeDtypeStruct((s, D_MODEL), F32),
        scratch_shapes=[pltpu.VMEM((tm, D_MODEL), BF16)],
        compiler_params=_cparams("parallel", "arbitrary"),
        name="ffn",
    )(x, gain, w_gate_up, w_gate_up, w_down)


def _memkv_kernel(mem_ref, g_ref, w_ref, kg_ref, mk_ref, mv_ref):
    h = _rms(mem_ref[...], g_ref[...]).astype(BF16)
    kv = jnp.dot(h, w_ref[...], preferred_element_type=F32)
    for hh in range(N_MEM_HEADS):
        sl = slice(hh * HEAD_DIM, (hh + 1) * HEAD_DIM)
        mk_ref[:, sl] = _rms(kv[:, sl], kg_ref[...]).astype(BF16)
    mv_ref[...] = kv[:, MEM_WIDTH:].astype(BF16)


def _memkv(mem, gain, w_mem_kv, k_gain, layer):
    m = mem.shape[0]
    const = lambda i: (0, 0)
    return pl.pallas_call(
        _memkv_kernel,
        grid=(1,),
        in_specs=[pl.BlockSpec((m, D_MODEL), const),
                  pl.BlockSpec((1, D_MODEL), const),
                  pl.BlockSpec((None, D_MODEL, 2 * MEM_WIDTH), lambda i: (layer, 0, 0)),
                  pl.BlockSpec((1, HEAD_DIM), const)],
        out_specs=[pl.BlockSpec((m, MEM_WIDTH), const)] * 2,
        out_shape=[jax.ShapeDtypeStruct((m, MEM_WIDTH), BF16)] * 2,
        compiler_params=_cparams("arbitrary"),
        name="memkv",
    )(mem, gain, w_mem_kv, k_gain)


def _mix_kernel(x_ref, g_ref, w_ref, qg_ref, kg_ref, cw_ref, cos_ref, sin_ref, mqg_ref,
                mk_ref, mv_ref, qT_ref, k_ref, vT_ref, ycm_ref, bound_ref,
                kmean_ref, carry_ref, kmax_ref, *, tm):
    i = pl.program_id(0)
    blocks_per_tile = tm // MOBA_BLOCK
    n_blk = kmean_ref.shape[1]

    @pl.when(i == 0)
    def _():
        kmean_ref[...] = jnp.zeros_like(kmean_ref)
        carry_ref[...] = jnp.zeros_like(carry_ref)
        kmax_ref[...] = jnp.zeros_like(kmax_ref)

    h = _rms(x_ref[...], g_ref[...]).astype(BF16)
    cos = cos_ref[...]
    sin = sin_ref[...]

    def rope(t):
        return t * cos + pltpu.roll(t, HEAD_DIM // 2, 1) * sin

    c0 = 3 * ATTN_WIDTH
    pk = jnp.dot(h, w_ref[:, ATTN_WIDTH:2 * ATTN_WIDTH], preferred_element_type=F32)
    pq = jnp.dot(h, w_ref[:, :ATTN_WIDTH], preferred_element_type=F32)
    pv = jnp.dot(h, w_ref[:, 2 * ATTN_WIDTH:c0], preferred_element_type=F32)
    pc = jnp.dot(h, w_ref[:, c0:c0 + 3 * CONV_CH], preferred_element_type=F32)
    pm = jnp.dot(h, w_ref[:, c0 + 3 * CONV_CH:], preferred_element_type=F32)
    slot = lax.broadcasted_iota(jnp.int32, (tm, HEAD_DIM), 1)
    row_blk = (i * tm + lax.broadcasted_iota(jnp.int32, (tm, HEAD_DIM), 0)) // MOBA_BLOCK
    one_hot = (slot == row_blk).astype(BF16)
    for hh in range(N_ATTN_HEADS):
        kr = rope(_rms(pk[:, hh * HEAD_DIM:(hh + 1) * HEAD_DIM], kg_ref[...]))
        kb = kr.astype(BF16)
        k_ref[hh, :, :HEAD_DIM] = kb
        k_ref[hh, :, HEAD_DIM:] = one_hot
        kn2 = jnp.max(jnp.sum(jnp.square(kb.astype(F32)), axis=1, keepdims=True), axis=0, keepdims=True)
        kmax_ref[hh:hh + 1, :] = jnp.maximum(kmax_ref[hh:hh + 1, :], kn2)
        for b in range(blocks_per_tile):
            blk = kr[b * MOBA_BLOCK:(b + 1) * MOBA_BLOCK]
            kmean_ref[hh, pl.ds(i * blocks_per_tile + b, 1), :] = jnp.mean(blk, axis=0, keepdims=True)

    q_blk = (i * tm + lax.broadcasted_iota(jnp.int32, (1, tm), 1)) // MOBA_BLOCK
    blk_id = lax.broadcasted_iota(jnp.int32, (n_blk, tm), 0)
    past = blk_id < q_blk
    for hh in range(N_ATTN_HEADS):
        qr = rope(_rms(pq[:, hh * HEAD_DIM:(hh + 1) * HEAD_DIM], qg_ref[...]))
        qT = qr.T
        gate = jnp.dot(kmean_ref[hh], qT, preferred_element_type=F32,
                       precision=lax.Precision.HIGHEST)
        gate = jnp.where(past, gate, -jnp.inf)
        selected = blk_id == q_blk
        for _ in range(MOBA_TOPK):
            best = jnp.max(gate, axis=0, keepdims=True)
            first = jnp.min(jnp.where(gate == best, blk_id, n_blk), axis=0, keepdims=True)
            pick = (blk_id == first) & (best > -jnp.inf)
            selected = selected | pick
            gate = jnp.where(pick, -jnp.inf, gate)
        qs = (qT * (HEAD_DIM ** -0.5 * LOG2E)).astype(BF16)
        qT_ref[hh, :HEAD_DIM, :] = qs
        qn2 = jnp.sum(jnp.square(qs.astype(F32)), axis=0, keepdims=True)
        bound = jnp.sqrt(qn2 * kmax_ref[hh:hh + 1, 0:1]) * BOUND_SLACK
        bound_ref[hh] = jnp.broadcast_to(bound, (8, tm))
        qT_ref[hh, HEAD_DIM:HEAD_DIM + n_blk, :] = jnp.where(selected, 0.0, MASK_BIAS).astype(BF16)
        qT_ref[hh, HEAD_DIM + n_blk:, :] = jnp.zeros((AUG_DIM - HEAD_DIM - n_blk, tm), BF16)

    ones_row = (lax.broadcasted_iota(jnp.int32, (V_ROWS - HEAD_DIM, tm), 0) == 0).astype(BF16)
    for hh in range(N_ATTN_HEADS):
        vT_ref[hh, 0, :HEAD_DIM, :] = pv[:, hh * HEAD_DIM:(hh + 1) * HEAD_DIM].T.astype(BF16)
        vT_ref[hh, 0, HEAD_DIM:, :] = ones_row

    u = pc[:, CONV_CH:2 * CONV_CH] * pc[:, 2 * CONV_CH:]
    prev1 = carry_ref[7:8, :]
    prev2 = carry_ref[6:7, :]
    row = lax.broadcasted_iota(jnp.int32, u.shape, 0)
    u1 = jnp.where(row == 0, prev1, pltpu.roll(u, 1, 0))
    u2 = jnp.where(row == 0, prev2, jnp.where(row == 1, prev1, pltpu.roll(u, 2, 0)))
    carry_ref[...] = u[tm - 8:, :]
    conv = u2 * cw_ref[0:1, :] + u1 * cw_ref[1:2, :] + u * cw_ref[2:3, :]
    ycm_ref[:, :CONV_CH] = (pc[:, :CONV_CH] * conv).astype(BF16)

    for hh in range(N_MEM_HEADS):
        sl = slice(hh * HEAD_DIM, (hh + 1) * HEAD_DIM)
        mq = (_rms(pm[:, sl], mqg_ref[...]) * HEAD_DIM ** -0.5).astype(BF16)
        sc = lax.dot_general(mq, mk_ref[:, sl], (((1,), (1,)), ((), ())), preferred_element_type=F32)
        p = jnp.exp(sc - jnp.max(sc, axis=-1, keepdims=True))
        o = jnp.dot(p.astype(BF16), mv_ref[:, sl], preferred_element_type=F32)
        o = o / jnp.sum(p, axis=-1, keepdims=True)
        ycm_ref[:, CONV_CH + hh * HEAD_DIM:CONV_CH + (hh + 1) * HEAD_DIM] = o.astype(BF16)


def _mix(x, gain, w_in, q_gain, k_gain, conv_w, cos, sin, mq_gain, mk, mv, layer, *, tm):
    s = x.shape[0]
    n_blk = s // MOBA_BLOCK
    bpt = tm // MOBA_BLOCK
    m = mk.shape[0]
    const = lambda i: (0, 0)
    return pl.pallas_call(
        functools.partial(_mix_kernel, tm=tm),
        grid=(s // tm,),
        in_specs=[pl.BlockSpec((tm, D_MODEL), lambda i: (i, 0)),
                  _resident((1, D_MODEL), const),
                  _resident((None, D_MODEL, IN_WIDTH), lambda i: (layer, 0, 0)),
                  _resident((1, HEAD_DIM), const),
                  _resident((1, HEAD_DIM), const),
                  _resident((CONV_K, CONV_CH), const),
                  pl.BlockSpec((tm, HEAD_DIM), lambda i: (i, 0)),
                  pl.BlockSpec((tm, HEAD_DIM), lambda i: (i, 0)),
                  _resident((1, HEAD_DIM), const),
                  _resident((m, MEM_WIDTH), const),
                  _resident((m, MEM_WIDTH), const)],
        out_specs=[pl.BlockSpec((N_ATTN_HEADS, AUG_DIM, tm), lambda i: (0, 0, i)),
                   pl.BlockSpec((N_ATTN_HEADS, tm, AUG_DIM), lambda i: (0, i, 0)),
                   pl.BlockSpec((N_ATTN_HEADS, 1, V_ROWS, tm), lambda i: (0, i, 0, 0)),
                   pl.BlockSpec((tm, CONV_CH + MEM_WIDTH), lambda i: (i, 0)),
                   pl.BlockSpec((N_ATTN_HEADS, 8, tm), lambda i: (0, 0, i))],
        out_shape=[jax.ShapeDtypeStruct((N_ATTN_HEADS, AUG_DIM, s), BF16),
                   jax.ShapeDtypeStruct((N_ATTN_HEADS, s, AUG_DIM), BF16),
                   jax.ShapeDtypeStruct((N_ATTN_HEADS, s // tm, V_ROWS, tm), BF16),
                   jax.ShapeDtypeStruct((s, CONV_CH + MEM_WIDTH), BF16),
                   jax.ShapeDtypeStruct((N_ATTN_HEADS, 8, s), F32)],
        scratch_shapes=[pltpu.VMEM((N_ATTN_HEADS, n_blk, HEAD_DIM), F32),
                        pltpu.VMEM((8, CONV_CH), F32),
                        pltpu.VMEM((N_ATTN_HEADS, HEAD_DIM), F32)],
        compiler_params=_cparams("arbitrary"),
        name="mix_proj",
    )(x, gain, w_in, q_gain, k_gain, conv_w, cos, sin, mq_gain, mk, mv)


Q_TILE = 2048
KEY_TILE = 512
M_INIT = 2 * MASK_BIAS
SAFE_LOGIT_BOUND = 40.0
BOUND_SLACK = 1.0 + 2.0 ** -7


def _causal_fix(s, first_lane_block):
    rows = []
    for b in range(s.shape[0] // MOBA_BLOCK):
        sb = s[b * MOBA_BLOCK:(b + 1) * MOBA_BLOCK]
        lo = (first_lane_block + b) * MOBA_BLOCK
        hi = lo + MOBA_BLOCK
        own = sb[:, lo:hi]
        key_pos = lax.broadcasted_iota(jnp.int32, own.shape, 0)
        qry_pos = lax.broadcasted_iota(jnp.int32, own.shape, 1)
        own = jnp.where(key_pos <= qry_pos, own, MASK_BIAS)
        pieces = ([sb[:, :lo]] if lo else []) + [own] + ([sb[:, hi:]] if hi < s.shape[1] else [])
        rows.append(jnp.concatenate(pieces, axis=1))
    return jnp.concatenate(rows, axis=0)


def _attn_kernel(qT_ref, k_ref, vT_ref, bound_ref, o_ref, acc_ref, *, q_tile, key_tile):
    t = pl.program_id(1)
    qT = qT_ref[0]
    acc_ref[...] = jnp.zeros_like(acc_ref)
    steps_per_tile = q_tile // key_tile
    bound = bound_ref[0, 0:1, :]
    bounded = jnp.max(bound) < SAFE_LOGIT_BOUND

    def scores(g, lane0, diagonal):
        rows = pl.ds(pl.multiple_of(g * key_tile, key_tile), key_tile)
        s = jnp.dot(k_ref[0, rows, :], qT[:, lane0:], preferred_element_type=F32)
        return _causal_fix(s, 0) if diagonal else s

    def update(g, s, m, lane0):
        m_old = m[:, lane0:]
        m_new = jnp.maximum(m_old, jnp.max(s, axis=0, keepdims=True))
        p = jnp.exp2(s - m_new).astype(BF16)
        acc_ref[:, lane0:] = (acc_ref[:, lane0:] * jnp.exp2(m_old - m_new)
                              + jnp.dot(vT_ref[0, g], p, preferred_element_type=F32))
        return jnp.concatenate([m[:, :lane0], m_new], axis=1) if lane0 else m_new

    def key_tiles(first, m, diagonal):
        tiles = [(first + d, d * key_tile if diagonal else 0) for d in range(steps_per_tile)]
        ss = [scores(g, lane0, diagonal) for g, lane0 in tiles]
        for (g, lane0), s in zip(tiles, ss):
            m = update(g, s, m, lane0)
        return m

    def bounded_tile(g, lane0, diagonal):
        p = jnp.exp2(scores(g, lane0, diagonal) - bound[:, lane0:]).astype(BF16)
        acc_ref[:, lane0:] += jnp.dot(vT_ref[0, g], p, preferred_element_type=F32)

    def bounded_past(u, carry):
        for d in range(steps_per_tile):
            bounded_tile(u * steps_per_tile + d, 0, False)
        return carry

    @pl.when(bounded)
    def _():
        lax.fori_loop(0, t, bounded_past, 0)
        for d in range(steps_per_tile):
            bounded_tile(t * steps_per_tile + d, d * key_tile, True)

    @pl.when(jnp.logical_not(bounded))
    def _():
        m = jnp.full((1, q_tile), M_INIT, F32)
        m = lax.fori_loop(0, t, lambda u, m: key_tiles(u * steps_per_tile, m, False), m)
        key_tiles(t * steps_per_tile, m, True)

    out = acc_ref[:HEAD_DIM, :] / acc_ref[HEAD_DIM:HEAD_DIM + 1, :]
    o_ref[...] = out.T.astype(o_ref.dtype)


def _attention(qT, k, vT, bound):
    s = k.shape[1]
    key_tile = vT.shape[3]
    q_tile = min(Q_TILE, s)
    return pl.pallas_call(
        functools.partial(_attn_kernel, q_tile=q_tile, key_tile=key_tile),
        grid=(N_ATTN_HEADS, s // q_tile),
        in_specs=[pl.BlockSpec((1, AUG_DIM, q_tile), lambda h, t: (h, 0, t)),
                  _resident((1, s, AUG_DIM), lambda h, t: (h, 0, 0)),
                  _resident((1, s // key_tile, V_ROWS, key_tile), lambda h, t: (h, 0, 0, 0)),
                  pl.BlockSpec((1, 8, q_tile), lambda h, t: (h, 0, t))],
        out_specs=pl.BlockSpec((q_tile, HEAD_DIM), lambda h, t: (t, h)),
        out_shape=jax.ShapeDtypeStruct((s, ATTN_WIDTH), BF16),
        scratch_shapes=[pltpu.VMEM((V_ROWS, q_tile), F32)],
        compiler_params=_cparams("parallel", "arbitrary"),
        name="moba_attn",
    )(qT, k, vT, bound)


def _outproj_kernel(x_ref, ya_ref, ycm_ref, w_ref, o_ref):
    o_ref[...] = (x_ref[...]
                  + jnp.dot(ya_ref[...], w_ref[:ATTN_WIDTH], preferred_element_type=F32)
                  + jnp.dot(ycm_ref[...], w_ref[ATTN_WIDTH:], preferred_element_type=F32))


def _outproj(x, y_attn, y_cm, w_out, layer, *, tm):
    s = x.shape[0]
    return pl.pallas_call(
        _outproj_kernel,
        grid=(s // tm,),
        in_specs=[pl.BlockSpec((tm, D_MODEL), lambda i: (i, 0)),
                  pl.BlockSpec((tm, ATTN_WIDTH), lambda i: (i, 0)),
                  pl.BlockSpec((tm, CONV_CH + MEM_WIDTH), lambda i: (i, 0)),
                  _resident((None, D_MODEL, D_MODEL), lambda i: (layer, 0, 0))],
        out_specs=pl.BlockSpec((tm, D_MODEL), lambda i: (i, 0)),
        out_shape=jax.ShapeDtypeStruct((s, D_MODEL), F32),
        compiler_params=_cparams("parallel"),
        name="out_proj",
    )(x, y_attn, y_cm, w_out)


def kernel(x, mem, positions, ffn1_norm, ffn1_w_gate_up, ffn1_w_down, mix_norm, w_in, q_norm, k_norm,
           conv_w, mem_norm, w_mem_kv, mq_norm, mk_norm, w_out, ffn2_norm, ffn2_w_gate_up, ffn2_w_down):
    b, s, _ = x.shape
    assert b == 1 and s % MOBA_BLOCK == 0 and s // MOBA_BLOCK <= HEAD_DIM // 2
    depth = w_in.shape[0]
    tm = min(s, KEY_TILE)
    ffn = functools.partial(_ffn, tm=min(s, FFN_ROW_TILE), tf=512)

    w1gu, w1d, w2gu, w2d, w_in_b, w_out_b, w_mkv = (
        w.astype(BF16) for w in (ffn1_w_gate_up, ffn1_w_down, ffn2_w_gate_up, ffn2_w_down,
                                 w_in, w_out, w_mem_kv))
    cos, sin = _rope_tables(positions[0])
    xs = x[0]
    mem2 = mem[0]
    for l in range(depth):
        xs = ffn(xs, ffn1_norm[l][None], w1gu, w1d, l)
        mk, mv = _memkv(mem2, mem_norm[l][None], w_mkv, mk_norm[l][None], l)
        qT, k, vT, y_cm, bound = _mix(xs, mix_norm[l][None], w_in_b, q_norm[l][None],
                                      k_norm[l][None], conv_w[l], cos, sin, mq_norm[l][None],
                                      mk, mv, l, tm=tm)
        y_attn = _attention(qT, k, vT, bound)
        xs = _outproj(xs, y_attn, y_cm, w_out_b, l, tm=tm)
        xs = ffn(xs, ffn2_norm[l][None], w2gu, w2d, l)
    return xs[None]
```

```python
import functools
import math

import jax
import jax.numpy as jnp
from jax import lax
from jax.experimental import pallas as pl
from jax.experimental.pallas import tpu as pltpu

F32 = jnp.float32
BF16 = jnp.bfloat16

D_MODEL = 2048
HEAD_DIM = 128
ATTN_WIDTH = 1024
N_ATTN_HEADS = 8
CONV_CH = 512
CONV_K = 3
MEM_WIDTH = 512
N_MEM_HEADS = 4
IN_WIDTH = 3 * ATTN_WIDTH + 3 * CONV_CH + MEM_WIDTH
D_FF = 5632
FFN_RES = 0.5
MOBA_BLOCK = 256
MOBA_TOPK = 3
ROPE_THETA = 10000.0
RMS_EPS = 1e-6

V7X_VMEM_BYTES = 64 * 1024 * 1024
VMEM_LIMIT = V7X_VMEM_BYTES - 8 * 1024 * 1024

MASK_BIAS = -1e30
AUG_DIM = 2 * HEAD_DIM
LOG2E = math.log2(math.e)


def _rms(x, gain):
    return x * lax.rsqrt(jnp.mean(x * x, axis=-1, keepdims=True) + RMS_EPS) * gain


def _cparams(*sem):
    return pltpu.CompilerParams(dimension_semantics=sem, vmem_limit_bytes=VMEM_LIMIT)


def _resident(shape, index_map):
    return pl.BlockSpec(shape, index_map, pipeline_mode=pl.Buffered(1))


def _rope_kernel(pos_ref, inv_ref, cos_ref, sin_ref):
    ang = pos_ref[...].astype(F32) * inv_ref[...]
    lane = lax.broadcasted_iota(jnp.int32, ang.shape, 1)
    s = jnp.sin(ang)
    cos_ref[...] = jnp.cos(ang)
    sin_ref[...] = jnp.where(lane < HEAD_DIM // 2, -s, s)


def _rope_tables(positions):
    s = positions.shape[0]
    t = min(s, 2048)
    inv = ROPE_THETA ** (-jnp.arange(0, HEAD_DIM, 2, dtype=F32) / HEAD_DIM)
    inv = jnp.concatenate([inv, inv])[None, :]
    return pl.pallas_call(
        _rope_kernel,
        grid=(s // t,),
        in_specs=[pl.BlockSpec((t, 1), lambda i: (i, 0)),
                  pl.BlockSpec((1, HEAD_DIM), lambda i: (0, 0))],
        out_specs=[pl.BlockSpec((t, HEAD_DIM), lambda i: (i, 0))] * 2,
        out_shape=[jax.ShapeDtypeStruct((s, HEAD_DIM), F32)] * 2,
        compiler_params=_cparams("parallel"),
        name="rope_tables",
    )(positions[:, None], inv)


FFN_ROW_TILE = 1024


def _ffn_kernel(x_ref, g_ref, wg_ref, wu_ref, wd_ref, o_ref, h_ref):
    j = pl.program_id(1)

    def half_step(h):
        gate = jnp.dot(h, wg_ref[...], preferred_element_type=F32)
        up = jnp.dot(h, wu_ref[...], preferred_element_type=F32)
        act = (gate * jax.nn.sigmoid(gate) * up * FFN_RES).astype(BF16)
        return jnp.dot(act, wd_ref[...], preferred_element_type=F32)

    @pl.when(j == 0)
    def _():
        x = x_ref[...]
        h = _rms(x, g_ref[...]).astype(BF16)
        h_ref[...] = h
        o_ref[...] = x + half_step(h)

    @pl.when(j > 0)
    def _():
        o_ref[...] += half_step(h_ref[...])


def _ffn(x, gain, w_gate_up, w_down, layer, *, tm, tf):
    s = x.shape[0]
    n_f = D_FF // tf
    return pl.pallas_call(
        _ffn_kernel,
        grid=(s // tm, n_f),
        in_specs=[pl.BlockSpec((tm, D_MODEL), lambda i, j: (i, 0)),
                  pl.BlockSpec((1, D_MODEL), lambda i, j: (0, 0)),
                  pl.BlockSpec((None, D_MODEL, tf), lambda i, j: (layer, 0, j)),
                  pl.BlockSpec((None, D_MODEL, tf), lambda i, j: (layer, 0, j + n_f)),
                  pl.BlockSpec((None, tf, D_MODEL), lambda i, j: (layer, j, 0))],
        out_specs=pl.BlockSpec((tm, D_MODEL), lambda i, j: (i, 0)),
        out_shape=jax.ShapeDtypeStruct((s, D_MODEL), F32),
        scratch_shapes=[pltpu.VMEM((tm, D_MODEL), BF16)],
        compiler_params=_cparams("parallel", "arbitrary"),
        name="ffn",
    )(x, gain, w_gate_up, w_gate_up, w_down)


def _memkv_kernel(mem_ref, g_ref, w_ref, kg_ref, mk_ref, mv_ref):
    h = _rms(mem_ref[...], g_ref[...]).astype(BF16)
    kv = jnp.dot(h, w_ref[...], preferred_element_type=F32)
    for hh in range(N_MEM_HEADS):
        sl = slice(hh * HEAD_DIM, (hh + 1) * HEAD_DIM)
        mk_ref[:, sl] = _rms(kv[:, sl], kg_ref[...]).astype(BF16)
    mv_ref[...] = kv[:, MEM_WIDTH:].astype(BF16)


def _memkv(mem, gain, w_mem_kv, k_gain, layer):
    m = mem.shape[0]
    const = lambda i: (0, 0)
    return pl.pallas_call(
        _memkv_kernel,
        grid=(1,),
        in_specs=[pl.BlockSpec((m, D_MODEL), const),
                  pl.BlockSpec((1, D_MODEL), const),
                  pl.BlockSpec((None, D_MODEL, 2 * MEM_WIDTH), lambda i: (layer, 0, 0)),
                  pl.BlockSpec((1, HEAD_DIM), const)],
        out_specs=[pl.BlockSpec((m, MEM_WIDTH), const)] * 2,
        out_shape=[jax.ShapeDtypeStruct((m, MEM_WIDTH), BF16)] * 2,
        compiler_params=_cparams("arbitrary"),
        name="memkv",
    )(mem, gain, w_mem_kv, k_gain)


def _mix_kernel(x_ref, g_ref, w_ref, qg_ref, kg_ref, cw_ref, cos_ref, sin_ref, mqg_ref,
                mk_ref, mv_ref, qT_ref, k_ref, vT_ref, ycm_ref, bound_ref,
                kmean_ref, carry_ref, kmax_ref, *, tm):
    i = pl.program_id(0)
    blocks_per_tile = tm // MOBA_BLOCK
    n_blk = kmean_ref.shape[1]

    @pl.when(i == 0)
    def _():
        kmean_ref[...] = jnp.zeros_like(kmean_ref)
        carry_ref[...] = jnp.zeros_like(carry_ref)
        kmax_ref[...] = jnp.zeros_like(kmax_ref)

    h = _rms(x_ref[...], g_ref[...]).astype(BF16)
    cos = cos_ref[...]
    sin = sin_ref[...]

    def rope(t):
        return t * cos + pltpu.roll(t, HEAD_DIM // 2, 1) * sin

    c0 = 3 * ATTN_WIDTH
    pk = jnp.dot(h, w_ref[:, ATTN_WIDTH:2 * ATTN_WIDTH], preferred_element_type=F32)
    pq = jnp.dot(h, w_ref[:, :ATTN_WIDTH], preferred_element_type=F32)
    pv = jnp.dot(h, w_ref[:, 2 * ATTN_WIDTH:c0], preferred_element_type=F32)
    pc = jnp.dot(h, w_ref[:, c0:c0 + 3 * CONV_CH], preferred_element_type=F32)
    pm = jnp.dot(h, w_ref[:, c0 + 3 * CONV_CH:], preferred_element_type=F32)
    slot = lax.broadcasted_iota(jnp.int32, (tm, HEAD_DIM), 1)
    row_blk = (i * tm + lax.broadcasted_iota(jnp.int32, (tm, HEAD_DIM), 0)) // MOBA_BLOCK
    one_hot = (slot == row_blk).astype(BF16)
    for hh in range(N_ATTN_HEADS):
        kr = rope(_rms(pk[:, hh * HEAD_DIM:(hh + 1) * HEAD_DIM], kg_ref[...]))
        kb = kr.astype(BF16)
        k_ref[hh, :, :HEAD_DIM] = kb
        k_ref[hh, :, HEAD_DIM:] = one_hot
        kn2 = jnp.max(jnp.sum(jnp.square(kb.astype(F32)), axis=1, keepdims=True), axis=0, keepdims=True)
        kmax_ref[hh:hh + 1, :] = jnp.maximum(kmax_ref[hh:hh + 1, :], kn2)
        for b in range(blocks_per_tile):
            blk = kr[b * MOBA_BLOCK:(b + 1) * MOBA_BLOCK]
            kmean_ref[hh, pl.ds(i * blocks_per_tile + b, 1), :] = jnp.mean(blk, axis=0, keepdims=True)

    q_blk = (i * tm + lax.broadcasted_iota(jnp.int32, (1, tm), 1)) // MOBA_BLOCK
    blk_id = lax.broadcasted_iota(jnp.int32, (n_blk, tm), 0)
    past = blk_id < q_blk
    for hh in range(N_ATTN_HEADS):
        qr = rope(_rms(pq[:, hh * HEAD_DIM:(hh + 1) * HEAD_DIM], qg_ref[...]))
        qT = qr.T
        gate = jnp.dot(kmean_ref[hh], qT, preferred_element_type=F32,
                       precision=lax.Precision.HIGHEST)
        gate = jnp.where(past, gate, -jnp.inf)
        bias = jnp.where(blk_id == q_blk, 0.0, MASK_BIAS)
        for _ in range(MOBA_TOPK):
            best = jnp.max(gate, axis=0, keepdims=True)
            first = jnp.min(jnp.where(gate == best, blk_id, n_blk), axis=0, keepdims=True)
            first = jnp.where(best > -jnp.inf, first, -1)
            pick = blk_id == first
            bias = jnp.where(pick, 0.0, bias)
            gate = jnp.where(pick, -jnp.inf, gate)
        qs = (qT * (HEAD_DIM ** -0.5 * LOG2E)).astype(BF16)
        qT_ref[hh, :HEAD_DIM, :] = qs
        qn2 = jnp.sum(jnp.square(qs.astype(F32)), axis=0, keepdims=True)
        bound = jnp.sqrt(qn2 * kmax_ref[hh:hh + 1, 0:1]) * BOUND_SLACK
        bound_ref[hh] = jnp.broadcast_to(bound, (8, tm))
        qT_ref[hh, HEAD_DIM:HEAD_DIM + n_blk, :] = bias.astype(BF16)
        qT_ref[hh, HEAD_DIM + n_blk:, :] = jnp.zeros((AUG_DIM - HEAD_DIM - n_blk, tm), BF16)

    for hh in range(N_ATTN_HEADS):
        vT_ref[hh, 0] = pv[:, hh * HEAD_DIM:(hh + 1) * HEAD_DIM].T.astype(BF16)

    u = pc[:, CONV_CH:2 * CONV_CH] * pc[:, 2 * CONV_CH:]
    prev1 = carry_ref[7:8, :]
    prev2 = carry_ref[6:7, :]
    row = lax.broadcasted_iota(jnp.int32, u.shape, 0)
    u1 = jnp.where(row == 0, prev1, pltpu.roll(u, 1, 0))
    u2 = jnp.where(row == 0, prev2, jnp.where(row == 1, prev1, pltpu.roll(u, 2, 0)))
    carry_ref[...] = u[tm - 8:, :]
    conv = u2 * cw_ref[0:1, :] + u1 * cw_ref[1:2, :] + u * cw_ref[2:3, :]
    ycm_ref[:, :CONV_CH] = (pc[:, :CONV_CH] * conv).astype(BF16)

    for hh in range(N_MEM_HEADS):
        sl = slice(hh * HEAD_DIM, (hh + 1) * HEAD_DIM)
        mq = (_rms(pm[:, sl], mqg_ref[...]) * HEAD_DIM ** -0.5).astype(BF16)
        sc = lax.dot_general(mq, mk_ref[:, sl], (((1,), (1,)), ((), ())), preferred_element_type=F32)
        p = jnp.exp(sc - jnp.max(sc, axis=-1, keepdims=True))
        o = jnp.dot(p.astype(BF16), mv_ref[:, sl], preferred_element_type=F32)
        o = o / jnp.sum(p, axis=-1, keepdims=True)
        ycm_ref[:, CONV_CH + hh * HEAD_DIM:CONV_CH + (hh + 1) * HEAD_DIM] = o.astype(BF16)


def _mix(x, gain, w_in, q_gain, k_gain, conv_w, cos, sin, mq_gain, mk, mv, layer, *, tm):
    s = x.shape[0]
    n_blk = s // MOBA_BLOCK
    m = mk.shape[0]
    const = lambda i: (0, 0)
    return pl.pallas_call(
        functools.partial(_mix_kernel, tm=tm),
        grid=(s // tm,),
        in_specs=[pl.BlockSpec((tm, D_MODEL), lambda i: (i, 0)),
                  _resident((1, D_MODEL), const),
                  _resident((None, D_MODEL, IN_WIDTH), lambda i: (layer, 0, 0)),
                  _resident((1, HEAD_DIM), const),
                  _resident((1, HEAD_DIM), const),
                  _resident((CONV_K, CONV_CH), const),
                  pl.BlockSpec((tm, HEAD_DIM), lambda i: (i, 0)),
                  pl.BlockSpec((tm, HEAD_DIM), lambda i: (i, 0)),
                  _resident((1, HEAD_DIM), const),
                  _resident((m, MEM_WIDTH), const),
                  _resident((m, MEM_WIDTH), const)],
        out_specs=[pl.BlockSpec((N_ATTN_HEADS, AUG_DIM, tm), lambda i: (0, 0, i)),
                   pl.BlockSpec((N_ATTN_HEADS, tm, AUG_DIM), lambda i: (0, i, 0)),
                   pl.BlockSpec((N_ATTN_HEADS, 1, HEAD_DIM, tm), lambda i: (0, i, 0, 0)),
                   pl.BlockSpec((tm, CONV_CH + MEM_WIDTH), lambda i: (i, 0)),
                   pl.BlockSpec((N_ATTN_HEADS, 8, tm), lambda i: (0, 0, i))],
        out_shape=[jax.ShapeDtypeStruct((N_ATTN_HEADS, AUG_DIM, s), BF16),
                   jax.ShapeDtypeStruct((N_ATTN_HEADS, s, AUG_DIM), BF16),
                   jax.ShapeDtypeStruct((N_ATTN_HEADS, s // tm, HEAD_DIM, tm), BF16),
                   jax.ShapeDtypeStruct((s, CONV_CH + MEM_WIDTH), BF16),
                   jax.ShapeDtypeStruct((N_ATTN_HEADS, 8, s), F32)],
        scratch_shapes=[pltpu.VMEM((N_ATTN_HEADS, n_blk, HEAD_DIM), F32),
                        pltpu.VMEM((8, CONV_CH), F32),
                        pltpu.VMEM((N_ATTN_HEADS, HEAD_DIM), F32)],
        compiler_params=_cparams("arbitrary"),
        name="mix_proj",
    )(x, gain, w_in, q_gain, k_gain, conv_w, cos, sin, mq_gain, mk, mv)


Q_TILE = 2048
KEY_TILE = 512
M_INIT = 2 * MASK_BIAS
SAFE_LOGIT_BOUND = 40.0
BOUND_SLACK = 1.0 + 2.0 ** -7


def _causal_fix(s, first_lane_block):
    rows = []
    for b in range(s.shape[0] // MOBA_BLOCK):
        sb = s[b * MOBA_BLOCK:(b + 1) * MOBA_BLOCK]
        lo = (first_lane_block + b) * MOBA_BLOCK
        hi = lo + MOBA_BLOCK
        own = sb[:, lo:hi]
        key_pos = lax.broadcasted_iota(jnp.int32, own.shape, 0)
        qry_pos = lax.broadcasted_iota(jnp.int32, own.shape, 1)
        own = jnp.where(key_pos <= qry_pos, own, MASK_BIAS)
        pieces = ([sb[:, :lo]] if lo else []) + [own] + ([sb[:, hi:]] if hi < s.shape[1] else [])
        rows.append(jnp.concatenate(pieces, axis=1))
    return jnp.concatenate(rows, axis=0)


def _attn_kernel(qT_ref, k_ref, vT_ref, bound_ref, o_ref, acc_ref, l_ref, *, q_tile, key_tile):
    t = pl.program_id(1)
    qT = qT_ref[0]
    acc_ref[...] = jnp.zeros_like(acc_ref)
    l_ref[...] = jnp.zeros_like(l_ref)
    steps_per_tile = q_tile // key_tile
    bound = bound_ref[0, 0:1, :]
    bounded = jnp.max(bound) < SAFE_LOGIT_BOUND

    def scores(g, lane0, diagonal):
        rows = pl.ds(pl.multiple_of(g * key_tile, key_tile), key_tile)
        s = jnp.dot(k_ref[0, rows, :], qT[:, lane0:], preferred_element_type=F32)
        return _causal_fix(s, 0) if diagonal else s

    def update(g, s, m, lane0):
        m_old = m[:, lane0:]
        m_new = jnp.maximum(m_old, jnp.max(s, axis=0, keepdims=True))
        p = jnp.exp2(s - m_new)
        alpha = jnp.exp2(m_old - m_new)
        l_ref[:, lane0:] = l_ref[:, lane0:] * alpha + jnp.sum(p, axis=0, keepdims=True)
        acc_ref[:, lane0:] = (acc_ref[:, lane0:] * alpha
                              + jnp.dot(vT_ref[0, g], p.astype(BF16), preferred_element_type=F32))
        return jnp.concatenate([m[:, :lane0], m_new], axis=1) if lane0 else m_new

    def key_tiles(first, m, diagonal):
        tiles = [(first + d, d * key_tile if diagonal else 0) for d in range(steps_per_tile)]
        ss = [scores(g, lane0, diagonal) for g, lane0 in tiles]
        for (g, lane0), s in zip(tiles, ss):
            m = update(g, s, m, lane0)
        return m

    def bounded_tile(g, lane0, diagonal):
        p = jnp.exp2(scores(g, lane0, diagonal) - bound[:, lane0:])
        l_ref[:, lane0:] += jnp.sum(p, axis=0, keepdims=True)
        acc_ref[:, lane0:] += jnp.dot(vT_ref[0, g], p.astype(BF16), preferred_element_type=F32)

    def bounded_past(u, carry):
        for d in range(steps_per_tile):
            bounded_tile(u * steps_per_tile + d, 0, False)
        return carry

    @pl.when(bounded)
    def _():
        lax.fori_loop(0, t, bounded_past, 0)
        for d in range(steps_per_tile):
            bounded_tile(t * steps_per_tile + d, d * key_tile, True)

    @pl.when(jnp.logical_not(bounded))
    def _():
        m = jnp.full((1, q_tile), M_INIT, F32)
        m = lax.fori_loop(0, t, lambda u, m: key_tiles(u * steps_per_tile, m, False), m)
        key_tiles(t * steps_per_tile, m, True)

    out = acc_ref[...] / l_ref[...]
    o_ref[...] = out.T.astype(o_ref.dtype)


def _attention(qT, k, vT, bound):
    s = k.shape[1]
    key_tile = vT.shape[3]
    q_tile = min(Q_TILE, s)
    return pl.pallas_call(
        functools.partial(_attn_kernel, q_tile=q_tile, key_tile=key_tile),
        grid=(N_ATTN_HEADS, s // q_tile),
        in_specs=[pl.BlockSpec((1, AUG_DIM, q_tile), lambda h, t: (h, 0, t)),
                  pl.BlockSpec((1, s, AUG_DIM), lambda h, t: (h, 0, 0)),
                  pl.BlockSpec((1, s // key_tile, HEAD_DIM, key_tile), lambda h, t: (h, 0, 0, 0)),
                  pl.BlockSpec((1, 8, q_tile), lambda h, t: (h, 0, t))],
        out_specs=pl.BlockSpec((q_tile, HEAD_DIM), lambda h, t: (t, h)),
        out_shape=jax.ShapeDtypeStruct((s, ATTN_WIDTH), BF16),
        scratch_shapes=[pltpu.VMEM((HEAD_DIM, q_tile), F32), pltpu.VMEM((1, q_tile), F32)],
        compiler_params=_cparams("parallel", "arbitrary"),
        name="moba_attn",
    )(qT, k, vT, bound)


def _outproj_kernel(x_ref, ya_ref, ycm_ref, w_ref, o_ref):
    o_ref[...] = (x_ref[...]
                  + jnp.dot(ya_ref[...], w_ref[:ATTN_WIDTH], preferred_element_type=F32)
                  + jnp.dot(ycm_ref[...], w_ref[ATTN_WIDTH:], preferred_element_type=F32))


def _outproj(x, y_attn, y_cm, w_out, layer, *, tm):
    s = x.shape[0]
    return pl.pallas_call(
        _outproj_kernel,
        grid=(s // tm,),
        in_specs=[pl.BlockSpec((tm, D_MODEL), lambda i: (i, 0)),
                  pl.BlockSpec((tm, ATTN_WIDTH), lambda i: (i, 0)),
                  pl.BlockSpec((tm, CONV_CH + MEM_WIDTH), lambda i: (i, 0)),
                  _resident((None, D_MODEL, D_MODEL), lambda i: (layer, 0, 0))],
        out_specs=pl.BlockSpec((tm, D_MODEL), lambda i: (i, 0)),
        out_shape=jax.ShapeDtypeStruct((s, D_MODEL), F32),
        compiler_params=_cparams("parallel"),
        name="out_proj",
    )(x, y_attn, y_cm, w_out)


def kernel(x, mem, positions, ffn1_norm, ffn1_w_gate_up, ffn1_w_down, mix_norm, w_in, q_norm, k_norm,
           conv_w, mem_norm, w_mem_kv, mq_norm, mk_norm, w_out, ffn2_norm, ffn2_w_gate_up, ffn2_w_down):
    b, s, _ = x.shape
    assert b == 1 and s % MOBA_BLOCK == 0 and s // MOBA_BLOCK <= HEAD_DIM // 2
    depth = w_in.shape[0]
    tm = min(s, KEY_TILE)
    ffn = functools.partial(_ffn, tm=min(s, FFN_ROW_TILE), tf=512)

    w1gu, w1d, w2gu, w2d, w_in_b, w_out_b, w_mkv = (
        w.astype(BF16) for w in (ffn1_w_gate_up, ffn1_w_down, ffn2_w_gate_up, ffn2_w_down,
                                 w_in, w_out, w_mem_kv))
    cos, sin = _rope_tables(positions[0])
    xs = x[0]
    mem2 = mem[0]
    for l in range(depth):
        xs = ffn(xs, ffn1_norm[l][None], w1gu, w1d, l)
        mk, mv = _memkv(mem2, mem_norm[l][None], w_mkv, mk_norm[l][None], l)
        qT, k, vT, y_cm, bound = _mix(xs, mix_norm[l][None], w_in_b, q_norm[l][None],
                                      k_norm[l][None], conv_w[l], cos, sin, mq_norm[l][None],
                                      mk, mv, l, tm=tm)
        y_attn = _attention(qT, k, vT, bound)
        xs = _outproj(xs, y_attn, y_cm, w_out_b, l, tm=tm)
        xs = ffn(xs, ffn2_norm[l][None], w2gu, w2d, l)
    return xs[None]
```

```python
import functools
import math

import jax
import jax.numpy as jnp
from jax import lax
from jax.experimental import pallas as pl
from jax.experimental.pallas import tpu as pltpu

F32 = jnp.float32
BF16 = jnp.bfloat16

D_MODEL = 2048
HEAD_DIM = 128
ATTN_WIDTH = 1024
N_ATTN_HEADS = 8
CONV_CH = 512
CONV_K = 3
MEM_WIDTH = 512
N_MEM_HEADS = 4
IN_WIDTH = 3 * ATTN_WIDTH + 3 * CONV_CH + MEM_WIDTH
D_FF = 5632
FFN_RES = 0.5
MOBA_BLOCK = 256
MOBA_TOPK = 3
ROPE_THETA = 10000.0
RMS_EPS = 1e-6

V7X_VMEM_BYTES = 64 * 1024 * 1024
VMEM_LIMIT = V7X_VMEM_BYTES - 8 * 1024 * 1024

MASK_BIAS = -1e30
AUG_DIM = 2 * HEAD_DIM
LOG2E = math.log2(math.e)


def _rms(x, gain):
    return x * lax.rsqrt(jnp.mean(x * x, axis=-1, keepdims=True) + RMS_EPS) * gain


def _split_bf16(a):
    hi = a.astype(BF16)
    return hi, (a - hi.astype(F32)).astype(BF16)


def _cparams(*sem):
    return pltpu.CompilerParams(dimension_semantics=sem, vmem_limit_bytes=VMEM_LIMIT)


def _resident(shape, index_map):
    return pl.BlockSpec(shape, index_map, pipeline_mode=pl.Buffered(1))


def _rope_kernel(pos_ref, inv_ref, cos_ref, sin_ref):
    ang = pos_ref[...].astype(F32) * inv_ref[...]
    lane = lax.broadcasted_iota(jnp.int32, ang.shape, 1)
    s = jnp.sin(ang)
    cos_ref[...] = jnp.cos(ang)
    sin_ref[...] = jnp.where(lane < HEAD_DIM // 2, -s, s)


def _rope_tables(positions):
    s = positions.shape[0]
    t = min(s, 2048)
    inv = ROPE_THETA ** (-jnp.arange(0, HEAD_DIM, 2, dtype=F32) / HEAD_DIM)
    inv = jnp.concatenate([inv, inv])[None, :]
    return pl.pallas_call(
        _rope_kernel,
        grid=(s // t,),
        in_specs=[pl.BlockSpec((t, 1), lambda i: (i, 0)),
                  pl.BlockSpec((1, HEAD_DIM), lambda i: (0, 0))],
        out_specs=[pl.BlockSpec((t, HEAD_DIM), lambda i: (i, 0))] * 2,
        out_shape=[jax.ShapeDtypeStruct((s, HEAD_DIM), F32)] * 2,
        compiler_params=_cparams("parallel"),
        name="rope_tables",
    )(positions[:, None], inv)


FFN_ROW_TILE = 1024


def _ffn_kernel(x_ref, g_ref, wg_ref, wu_ref, wd_ref, o_ref, h_ref):
    j = pl.program_id(1)

    def half_step(h):
        gate = jnp.dot(h, wg_ref[...], preferred_element_type=F32)
        up = jnp.dot(h, wu_ref[...], preferred_element_type=F32)
        act = (gate * jax.nn.sigmoid(gate) * up * FFN_RES).astype(BF16)
        return jnp.dot(act, wd_ref[...], preferred_element_type=F32)

    @pl.when(j == 0)
    def _():
        x = x_ref[...]
        h = _rms(x, g_ref[...]).astype(BF16)
        h_ref[...] = h
        o_ref[...] = x + half_step(h)

    @pl.when(j > 0)
    def _():
        o_ref[...] += half_step(h_ref[...])


def _ffn(x, gain, w_gate_up, w_down, layer, *, tm, tf):
    s = x.shape[0]
    n_f = D_FF // tf
    return pl.pallas_call(
        _ffn_kernel,
        grid=(s // tm, n_f),
        in_specs=[pl.BlockSpec((tm, D_MODEL), lambda i, j: (i, 0)),
                  pl.BlockSpec((1, D_MODEL), lambda i, j: (0, 0)),
                  pl.BlockSpec((None, D_MODEL, tf), lambda i, j: (layer, 0, j)),
                  pl.BlockSpec((None, D_MODEL, tf), lambda i, j: (layer, 0, j + n_f)),
                  pl.BlockSpec((None, tf, D_MODEL), lambda i, j: (layer, j, 0))],
        out_specs=pl.BlockSpec((tm, D_MODEL), lambda i, j: (i, 0)),
        out_shape=jax.ShapeDtypeStruct((s, D_MODEL), F32),
        scratch_shapes=[pltpu.VMEM((tm, D_MODEL), BF16)],
        compiler_params=_cparams("parallel", "arbitrary"),
        name="ffn",
    )(x, gain, w_gate_up, w_gate_up, w_down)


def _memkv_kernel(mem_ref, g_ref, w_ref, kg_ref, mk_ref, mv_ref):
    h = _rms(mem_ref[...], g_ref[...]).astype(BF16)
    kv = jnp.dot(h, w_ref[...], preferred_element_type=F32)
    for hh in range(N_MEM_HEADS):
        sl = slice(hh * HEAD_DIM, (hh + 1) * HEAD_DIM)
        mk_ref[:, sl] = _rms(kv[:, sl], kg_ref[...]).astype(BF16)
    mv_ref[...] = kv[:, MEM_WIDTH:].astype(BF16)


def _memkv(mem, gain, w_mem_kv, k_gain, layer):
    m = mem.shape[0]
    const = lambda i: (0, 0)
    return pl.pallas_call(
        _memkv_kernel,
        grid=(1,),
        in_specs=[pl.BlockSpec((m, D_MODEL), const),
                  pl.BlockSpec((1, D_MODEL), const),
                  pl.BlockSpec((None, D_MODEL, 2 * MEM_WIDTH), lambda i: (layer, 0, 0)),
                  pl.BlockSpec((1, HEAD_DIM), const)],
        out_specs=[pl.BlockSpec((m, MEM_WIDTH), const)] * 2,
        out_shape=[jax.ShapeDtypeStruct((m, MEM_WIDTH), BF16)] * 2,
        compiler_params=_cparams("arbitrary"),
        name="memkv",
    )(mem, gain, w_mem_kv, k_gain)


def _mix_kernel(x_ref, g_ref, w_ref, qg_ref, kg_ref, cw_ref, cos_ref, sin_ref, mqg_ref,
                mk_ref, mv_ref, qT_ref, k_ref, vT_ref, ycm_ref, bound_ref,
                kmean_ref, carry_ref, kmax_ref, *, tm):
    i = pl.program_id(0)
    blocks_per_tile = tm // MOBA_BLOCK
    n_blk = kmean_ref.shape[1]

    @pl.when(i == 0)
    def _():
        kmean_ref[...] = jnp.zeros_like(kmean_ref)
        carry_ref[...] = jnp.zeros_like(carry_ref)
        kmax_ref[...] = jnp.zeros_like(kmax_ref)

    h = _rms(x_ref[...], g_ref[...]).astype(BF16)
    cos = cos_ref[...]
    sin = sin_ref[...]

    def rope(t):
        return t * cos + pltpu.roll(t, HEAD_DIM // 2, 1) * sin

    c0 = 3 * ATTN_WIDTH
    slot = lax.broadcasted_iota(jnp.int32, (tm, HEAD_DIM), 1)
    row_blk = (i * tm + lax.broadcasted_iota(jnp.int32, (tm, HEAD_DIM), 0)) // MOBA_BLOCK
    one_hot = (slot == row_blk).astype(BF16)
    q_blk = (i * tm + lax.broadcasted_iota(jnp.int32, (1, tm), 1)) // MOBA_BLOCK
    blk_id = lax.broadcasted_iota(jnp.int32, (n_blk, tm), 0)
    past = blk_id < q_blk

    def project(col, width=2 * HEAD_DIM):
        return jnp.dot(h, w_ref[:, col:col + width], preferred_element_type=F32)

    def key_head(pk_h, hh):
        kr = rope(_rms(pk_h, kg_ref[...]))
        kb = kr.astype(BF16)
        k_ref[hh, :, :HEAD_DIM] = kb
        k_ref[hh, :, HEAD_DIM:] = one_hot
        kn2 = jnp.max(jnp.sum(jnp.square(kb.astype(F32)), axis=1, keepdims=True), axis=0, keepdims=True)
        kmax_ref[hh:hh + 1, :] = jnp.maximum(kmax_ref[hh:hh + 1, :], kn2)
        for b in range(blocks_per_tile):
            blk = kr[b * MOBA_BLOCK:(b + 1) * MOBA_BLOCK]
            kmean_ref[hh, pl.ds(i * blocks_per_tile + b, 1), :] = jnp.mean(blk, axis=0, keepdims=True)

    def memory_head(pm_h, hh):
        sl = slice(hh * HEAD_DIM, (hh + 1) * HEAD_DIM)
        mq = (_rms(pm_h, mqg_ref[...]) * HEAD_DIM ** -0.5).astype(BF16)
        sc = lax.dot_general(mq, mk_ref[:, sl], (((1,), (1,)), ((), ())), preferred_element_type=F32)
        p = jnp.exp(sc - jnp.max(sc, axis=-1, keepdims=True))
        o = jnp.dot(p.astype(BF16), mv_ref[:, sl], preferred_element_type=F32)
        o = o / jnp.sum(p, axis=-1, keepdims=True)
        ycm_ref[:, CONV_CH + hh * HEAD_DIM:CONV_CH + (hh + 1) * HEAD_DIM] = o.astype(BF16)

    def query_head(pq_h, hh):
        qT = rope(_rms(pq_h, qg_ref[...])).T
        km_hi, km_lo = _split_bf16(kmean_ref[hh])
        q_hi, q_lo = _split_bf16(qT)
        gate = jnp.dot(jnp.concatenate([km_hi, km_hi, km_lo], axis=1),
                       jnp.concatenate([q_hi, q_lo, q_hi], axis=0),
                       preferred_element_type=F32)
        gate = jnp.where(past, gate, -jnp.inf)
        bias = jnp.where(blk_id == q_blk, 0.0, MASK_BIAS)
        for _ in range(MOBA_TOPK):
            best = jnp.max(gate, axis=0, keepdims=True)
            first = jnp.min(jnp.where(gate == best, blk_id, n_blk), axis=0, keepdims=True)
            first = jnp.where(best > -jnp.inf, first, -1)
            pick = blk_id == first
            bias = jnp.where(pick, 0.0, bias)
            gate = jnp.where(pick, -jnp.inf, gate)
        qs = (qT * (HEAD_DIM ** -0.5 * LOG2E)).astype(BF16)
        qT_ref[hh, :HEAD_DIM, :] = qs
        qn2 = jnp.sum(jnp.square(qs.astype(F32)), axis=0, keepdims=True)
        bound = jnp.sqrt(qn2 * kmax_ref[hh:hh + 1, 0:1]) * BOUND_SLACK
        bound_ref[hh] = jnp.broadcast_to(bound, (8, tm))
        qT_ref[hh, HEAD_DIM:HEAD_DIM + n_blk, :] = bias.astype(BF16)
        qT_ref[hh, HEAD_DIM + n_blk:, :] = jnp.zeros((AUG_DIM - HEAD_DIM - n_blk, tm), BF16)

    pk = project(ATTN_WIDTH, ATTN_WIDTH)
    pq = project(0, ATTN_WIDTH)
    for hh in range(N_ATTN_HEADS):
        key_head(pk[:, hh * HEAD_DIM:(hh + 1) * HEAD_DIM], hh)
    rest = []
    for hh in range(N_ATTN_HEADS):
        if hh % 2 == 0:
            pv2 = project(2 * ATTN_WIDTH + hh * HEAD_DIM)
            for d in range(2):
                vT_ref[hh + d, 0] = pv2[:, d * HEAD_DIM:(d + 1) * HEAD_DIM].T.astype(BF16)
        rest.append(project(c0 + hh * 2 * HEAD_DIM))
        query_head(pq[:, hh * HEAD_DIM:(hh + 1) * HEAD_DIM], hh)

    rest = jnp.concatenate(rest, axis=1)
    pc = rest[:, :3 * CONV_CH]
    pm = rest[:, 3 * CONV_CH:]

    u = pc[:, CONV_CH:2 * CONV_CH] * pc[:, 2 * CONV_CH:]
    prev1 = carry_ref[7:8, :]
    prev2 = carry_ref[6:7, :]
    row = lax.broadcasted_iota(jnp.int32, u.shape, 0)
    u1 = jnp.where(row == 0, prev1, pltpu.roll(u, 1, 0))
    u2 = jnp.where(row == 0, prev2, jnp.where(row == 1, prev1, pltpu.roll(u, 2, 0)))
    carry_ref[...] = u[tm - 8:, :]
    conv = u2 * cw_ref[0:1, :] + u1 * cw_ref[1:2, :] + u * cw_ref[2:3, :]
    ycm_ref[:, :CONV_CH] = (pc[:, :CONV_CH] * conv).astype(BF16)

    for hh in range(N_MEM_HEADS):
        memory_head(pm[:, hh * HEAD_DIM:(hh + 1) * HEAD_DIM], hh)


def _mix(x, gain, w_in, q_gain, k_gain, conv_w, cos, sin, mq_gain, mk, mv, layer, *, tm):
    s = x.shape[0]
    n_blk = s // MOBA_BLOCK
    m = mk.shape[0]
    const = lambda i: (0, 0)
    return pl.pallas_call(
        functools.partial(_mix_kernel, tm=tm),
        grid=(s // tm,),
        in_specs=[pl.BlockSpec((tm, D_MODEL), lambda i: (i, 0)),
                  _resident((1, D_MODEL), const),
                  _resident((None, D_MODEL, IN_WIDTH), lambda i: (layer, 0, 0)),
                  _resident((1, HEAD_DIM), const),
                  _resident((1, HEAD_DIM), const),
                  _resident((CONV_K, CONV_CH), const),
                  pl.BlockSpec((tm, HEAD_DIM), lambda i: (i, 0)),
                  pl.BlockSpec((tm, HEAD_DIM), lambda i: (i, 0)),
                  _resident((1, HEAD_DIM), const),
                  _resident((m, MEM_WIDTH), const),
                  _resident((m, MEM_WIDTH), const)],
        out_specs=[pl.BlockSpec((N_ATTN_HEADS, AUG_DIM, tm), lambda i: (0, 0, i)),
                   pl.BlockSpec((N_ATTN_HEADS, tm, AUG_DIM), lambda i: (0, i, 0)),
                   pl.BlockSpec((N_ATTN_HEADS, 1, HEAD_DIM, tm), lambda i: (0, i, 0, 0)),
                   pl.BlockSpec((tm, CONV_CH + MEM_WIDTH), lambda i: (i, 0)),
                   pl.BlockSpec((N_ATTN_HEADS, 8, tm), lambda i: (0, 0, i))],
        out_shape=[jax.ShapeDtypeStruct((N_ATTN_HEADS, AUG_DIM, s), BF16),
                   jax.ShapeDtypeStruct((N_ATTN_HEADS, s, AUG_DIM), BF16),
                   jax.ShapeDtypeStruct((N_ATTN_HEADS, s // tm, HEAD_DIM, tm), BF16),
                   jax.ShapeDtypeStruct((s, CONV_CH + MEM_WIDTH), BF16),
                   jax.ShapeDtypeStruct((N_ATTN_HEADS, 8, s), F32)],
        scratch_shapes=[pltpu.VMEM((N_ATTN_HEADS, n_blk, HEAD_DIM), F32),
                        pltpu.VMEM((8, CONV_CH), F32),
                        pltpu.VMEM((N_ATTN_HEADS, HEAD_DIM), F32)],
        compiler_params=_cparams("arbitrary"),
        name="mix_proj",
    )(x, gain, w_in, q_gain, k_gain, conv_w, cos, sin, mq_gain, mk, mv)


Q_TILE = 2048
KEY_TILE = 512
M_INIT = 2 * MASK_BIAS
SAFE_LOGIT_BOUND = 40.0
BOUND_SLACK = 1.0 + 2.0 ** -7


def _causal_fix(s, first_lane_block):
    rows = []
    for b in range(s.shape[0] // MOBA_BLOCK):
        sb = s[b * MOBA_BLOCK:(b + 1) * MOBA_BLOCK]
        lo = (first_lane_block + b) * MOBA_BLOCK
        hi = lo + MOBA_BLOCK
        own = sb[:, lo:hi]
        key_pos = lax.broadcasted_iota(jnp.int32, own.shape, 0)
        qry_pos = lax.broadcasted_iota(jnp.int32, own.shape, 1)
        own = jnp.where(key_pos <= qry_pos, own, MASK_BIAS)
        pieces = ([sb[:, :lo]] if lo else []) + [own] + ([sb[:, hi:]] if hi < s.shape[1] else [])
        rows.append(jnp.concatenate(pieces, axis=1))
    return jnp.concatenate(rows, axis=0)


def _attn_kernel(qT_ref, k_ref, vT_ref, bound_ref, o_ref, acc_ref, l_ref, *, q_tile, key_tile):
    t = pl.program_id(1)
    qT = qT_ref[0]
    acc_ref[...] = jnp.zeros_like(acc_ref)
    l_ref[...] = jnp.zeros_like(l_ref)
    steps_per_tile = q_tile // key_tile
    bound = bound_ref[0, 0:1, :]
    bounded = jnp.max(bound) < SAFE_LOGIT_BOUND

    def scores(g, lane0, diagonal):
        rows = pl.ds(pl.multiple_of(g * key_tile, key_tile), key_tile)
        s = jnp.dot(k_ref[0, rows, :], qT[:, lane0:], preferred_element_type=F32)
        return _causal_fix(s, 0) if diagonal else s

    def update(g, s, m, lane0):
        m_old = m[:, lane0:]
        m_new = jnp.maximum(m_old, jnp.max(s, axis=0, keepdims=True))
        p = jnp.exp2(s - m_new)
        alpha = jnp.exp2(m_old - m_new)
        l_ref[:, lane0:] = l_ref[:, lane0:] * alpha + jnp.sum(p, axis=0, keepdims=True)
        acc_ref[:, lane0:] = (acc_ref[:, lane0:] * alpha
                              + jnp.dot(vT_ref[0, g], p.astype(BF16), preferred_element_type=F32))
        return jnp.concatenate([m[:, :lane0], m_new], axis=1) if lane0 else m_new

    def key_tiles(first, m, diagonal):
        tiles = [(first + d, d * key_tile if diagonal else 0) for d in range(steps_per_tile)]
        ss = [scores(g, lane0, diagonal) for g, lane0 in tiles]
        for (g, lane0), s in zip(tiles, ss):
            m = update(g, s, m, lane0)
        return m

    def bounded_tile(g, lane0, diagonal):
        p = jnp.exp2(scores(g, lane0, diagonal) - bound[:, lane0:])
        l_ref[:, lane0:] += jnp.sum(p, axis=0, keepdims=True)
        acc_ref[:, lane0:] += jnp.dot(vT_ref[0, g], p.astype(BF16), preferred_element_type=F32)

    def bounded_past(u, carry):
        for d in range(steps_per_tile):
            bounded_tile(u * steps_per_tile + d, 0, False)
        return carry

    @pl.when(bounded)
    def _():
        lax.fori_loop(0, t, bounded_past, 0)
        for d in range(steps_per_tile):
            bounded_tile(t * steps_per_tile + d, d * key_tile, True)

    @pl.when(jnp.logical_not(bounded))
    def _():
        m = jnp.full((1, q_tile), M_INIT, F32)
        m = lax.fori_loop(0, t, lambda u, m: key_tiles(u * steps_per_tile, m, False), m)
        key_tiles(t * steps_per_tile, m, True)

    out = acc_ref[...] / l_ref[...]
    o_ref[...] = out.T.astype(o_ref.dtype)


def _attention(qT, k, vT, bound):
    s = k.shape[1]
    key_tile = vT.shape[3]
    q_tile = min(Q_TILE, s)
    return pl.pallas_call(
        functools.partial(_attn_kernel, q_tile=q_tile, key_tile=key_tile),
        grid=(N_ATTN_HEADS, s // q_tile),
        in_specs=[pl.BlockSpec((1, AUG_DIM, q_tile), lambda h, t: (h, 0, t)),
                  pl.BlockSpec((1, s, AUG_DIM), lambda h, t: (h, 0, 0)),
                  pl.BlockSpec((1, s // key_tile, HEAD_DIM, key_tile), lambda h, t: (h, 0, 0, 0)),
                  pl.BlockSpec((1, 8, q_tile), lambda h, t: (h, 0, t))],
        out_specs=pl.BlockSpec((q_tile, HEAD_DIM), lambda h, t: (t, h)),
        out_shape=jax.ShapeDtypeStruct((s, ATTN_WIDTH), BF16),
        scratch_shapes=[pltpu.VMEM((HEAD_DIM, q_tile), F32), pltpu.VMEM((1, q_tile), F32)],
        compiler_params=_cparams("parallel", "arbitrary"),
        name="moba_attn",
    )(qT, k, vT, bound)


def _outproj_kernel(x_ref, ya_ref, ycm_ref, w_ref, o_ref):
    o_ref[...] = (x_ref[...]
                  + jnp.dot(ya_ref[...], w_ref[:ATTN_WIDTH], preferred_element_type=F32)
                  + jnp.dot(ycm_ref[...], w_ref[ATTN_WIDTH:], preferred_element_type=F32))


def _outproj(x, y_attn, y_cm, w_out, layer, *, tm):
    s = x.shape[0]
    return pl.pallas_call(
        _outproj_kernel,
        grid=(s // tm,),
        in_specs=[pl.BlockSpec((tm, D_MODEL), lambda i: (i, 0)),
                  pl.BlockSpec((tm, ATTN_WIDTH), lambda i: (i, 0)),
                  pl.BlockSpec((tm, CONV_CH + MEM_WIDTH), lambda i: (i, 0)),
                  _resident((None, D_MODEL, D_MODEL), lambda i: (layer, 0, 0))],
        out_specs=pl.BlockSpec((tm, D_MODEL), lambda i: (i, 0)),
        out_shape=jax.ShapeDtypeStruct((s, D_MODEL), F32),
        compiler_params=_cparams("parallel"),
        name="out_proj",
    )(x, y_attn, y_cm, w_out)


def kernel(x, mem, positions, ffn1_norm, ffn1_w_gate_up, ffn1_w_down, mix_norm, w_in, q_norm, k_norm,
           conv_w, mem_norm, w_mem_kv, mq_norm, mk_norm, w_out, ffn2_norm, ffn2_w_gate_up, ffn2_w_down):
    b, s, _ = x.shape
    assert b == 1 and s % MOBA_BLOCK == 0 and s // MOBA_BLOCK <= HEAD_DIM // 2
    depth = w_in.shape[0]
    tm = min(s, KEY_TILE)
    ffn = functools.partial(_ffn, tm=min(s, FFN_ROW_TILE), tf=512)

    w1gu, w1d, w2gu, w2d, w_in_b, w_out_b, w_mkv = (
        w.astype(BF16) for w in (ffn1_w_gate_up, ffn1_w_down, ffn2_w_gate_up, ffn2_w_down,
                                 w_in, w_out, w_mem_kv))
    cos, sin = _rope_tables(positions[0])
    xs = x[0]
    mem2 = mem[0]
    for l in range(depth):
        xs = ffn(xs, ffn1_norm[l][None], w1gu, w1d, l)
        mk, mv = _memkv(mem2, mem_norm[l][None], w_mkv, mk_norm[l][None], l)
        qT, k, vT, y_cm, bound = _mix(xs, mix_norm[l][None], w_in_b, q_norm[l][None],
                                      k_norm[l][None], conv_w[l], cos, sin, mq_norm[l][None],
                                      mk, mv, l, tm=tm)
        y_attn = _attention(qT, k, vT, bound)
        xs = _outproj(xs, y_attn, y_cm, w_out_b, l, tm=tm)
        xs = ffn(xs, ffn2_norm[l][None], w2gu, w2d, l)
    return xs[None]
```

```python
import functools
import math

import jax
import jax.numpy as jnp
from jax import lax
from jax.experimental import pallas as pl
from jax.experimental.pallas import tpu as pltpu

F32 = jnp.float32
BF16 = jnp.bfloat16

D_MODEL = 2048
HEAD_DIM = 128
ATTN_WIDTH = 1024
N_ATTN_HEADS = 8
CONV_CH = 512
CONV_K = 3
MEM_WIDTH = 512
N_MEM_HEADS = 4
IN_WIDTH = 3 * ATTN_WIDTH + 3 * CONV_CH + MEM_WIDTH
D_FF = 5632
FFN_RES = 0.5
MOBA_BLOCK = 256
MOBA_TOPK = 3
ROPE_THETA = 10000.0
RMS_EPS = 1e-6

V7X_VMEM_BYTES = 64 * 1024 * 1024
VMEM_LIMIT = V7X_VMEM_BYTES - 6 * 1024 * 1024

MASK_BIAS = -1e30
AUG_DIM = 2 * HEAD_DIM
LOG2E = math.log2(math.e)


def _rms(x, gain):
    return x * lax.rsqrt(jnp.mean(x * x, axis=-1, keepdims=True) + RMS_EPS) * gain


def _split_bf16(a):
    hi = a.astype(BF16)
    return hi, (a - hi.astype(F32)).astype(BF16)


def _cparams(*sem):
    return pltpu.CompilerParams(dimension_semantics=sem, vmem_limit_bytes=VMEM_LIMIT)


def _resident(shape, index_map):
    return pl.BlockSpec(shape, index_map, pipeline_mode=pl.Buffered(1))


def _rope_kernel(pos_ref, inv_ref, cos_ref, sin_ref):
    ang = pos_ref[...].astype(F32) * inv_ref[...]
    lane = lax.broadcasted_iota(jnp.int32, ang.shape, 1)
    s = jnp.sin(ang)
    cos_ref[...] = jnp.cos(ang)
    sin_ref[...] = jnp.where(lane < HEAD_DIM // 2, -s, s)


def _rope_tables(positions):
    s = positions.shape[0]
    t = min(s, 2048)
    inv = ROPE_THETA ** (-jnp.arange(0, HEAD_DIM, 2, dtype=F32) / HEAD_DIM)
    inv = jnp.concatenate([inv, inv])[None, :]
    return pl.pallas_call(
        _rope_kernel,
        grid=(s // t,),
        in_specs=[pl.BlockSpec((t, 1), lambda i: (i, 0)),
                  pl.BlockSpec((1, HEAD_DIM), lambda i: (0, 0))],
        out_specs=[pl.BlockSpec((t, HEAD_DIM), lambda i: (i, 0))] * 2,
        out_shape=[jax.ShapeDtypeStruct((s, HEAD_DIM), F32)] * 2,
        compiler_params=_cparams("parallel"),
        name="rope_tables",
    )(positions[:, None], inv)


FFN_ROW_TILE = 1024


def _ffn_kernel(*refs, convert_next):
    if convert_next:
        x_ref, g_ref, wg_ref, wu_ref, wd_ref, ngu_ref, nd_ref, o_ref, ngu_out, nd_out, h_ref = refs
    else:
        x_ref, g_ref, wg_ref, wu_ref, wd_ref, o_ref, h_ref = refs
    j = pl.program_id(1)

    def half_step(h):
        gate = jnp.dot(h, wg_ref[...], preferred_element_type=F32)
        up = jnp.dot(h, wu_ref[...], preferred_element_type=F32)
        act = (gate * jax.nn.sigmoid(gate) * up * FFN_RES).astype(BF16)
        if convert_next:
            ngu_out[...] = ngu_ref[...].astype(BF16)
            nd_out[...] = nd_ref[...].astype(BF16)
        return jnp.dot(act, wd_ref[...], preferred_element_type=F32)

    @pl.when(j == 0)
    def _():
        x = x_ref[...]
        h = _rms(x, g_ref[...]).astype(BF16)
        h_ref[...] = h
        o_ref[...] = x + half_step(h)

    @pl.when(j > 0)
    def _():
        o_ref[...] += half_step(h_ref[...])


def _ffn(x, gain, w_gate_up, w_down, next_weights=None, *, tm, tf):
    s = x.shape[0]
    n_i, n_f = s // tm, D_FF // tf
    in_specs = [pl.BlockSpec((tm, D_MODEL), lambda i, j: (i, 0)),
                pl.BlockSpec((1, D_MODEL), lambda i, j: (0, 0)),
                pl.BlockSpec((D_MODEL, tf), lambda i, j: (0, j)),
                pl.BlockSpec((D_MODEL, tf), lambda i, j: (0, j + n_f)),
                pl.BlockSpec((tf, D_MODEL), lambda i, j: (j, 0))]
    out_specs = [pl.BlockSpec((tm, D_MODEL), lambda i, j: (i, 0))]
    out_shape = [jax.ShapeDtypeStruct((s, D_MODEL), F32)]
    operands = [x, gain, w_gate_up, w_gate_up, w_down]
    if next_weights is not None:
        gu32, d32, layer = next_weights
        gu_blk = (D_MODEL // n_i, 2 * D_FF // n_f)
        d_blk = (D_FF // (n_i * n_f), D_MODEL)
        in_specs += [pl.BlockSpec((None,) + gu_blk, lambda i, j: (layer, i, j)),
                     pl.BlockSpec((None,) + d_blk, lambda i, j: (layer, i * n_f + j, 0))]
        out_specs += [pl.BlockSpec(gu_blk, lambda i, j: (i, j)),
                      pl.BlockSpec(d_blk, lambda i, j: (i * n_f + j, 0))]
        out_shape += [jax.ShapeDtypeStruct(gu32.shape[1:], BF16), jax.ShapeDtypeStruct(d32.shape[1:], BF16)]
        operands += [gu32, d32]
    return pl.pallas_call(
        functools.partial(_ffn_kernel, convert_next=next_weights is not None),
        grid=(n_i, n_f),
        in_specs=in_specs,
        out_specs=out_specs,
        out_shape=out_shape,
        scratch_shapes=[pltpu.VMEM((tm, D_MODEL), BF16)],
        compiler_params=_cparams("parallel", "arbitrary"),
        name="ffn",
    )(*operands)


def _memkv_kernel(mem_ref, g_ref, w_ref, kg_ref, mk_ref, mv_ref):
    h = _rms(mem_ref[...], g_ref[...]).astype(BF16)
    kv = jnp.dot(h, w_ref[...], preferred_element_type=F32)
    for hh in range(N_MEM_HEADS):
        sl = slice(hh * HEAD_DIM, (hh + 1) * HEAD_DIM)
        mk_ref[:, sl] = _rms(kv[:, sl], kg_ref[...]).astype(BF16)
    mv_ref[...] = kv[:, MEM_WIDTH:].astype(BF16)


def _memkv(mem, gain, w_mem_kv, k_gain, layer):
    m = mem.shape[0]
    const = lambda i: (0, 0)
    return pl.pallas_call(
        _memkv_kernel,
        grid=(1,),
        in_specs=[pl.BlockSpec((m, D_MODEL), const),
                  pl.BlockSpec((1, D_MODEL), const),
                  pl.BlockSpec((None, D_MODEL, 2 * MEM_WIDTH), lambda i: (layer, 0, 0)),
                  pl.BlockSpec((1, HEAD_DIM), const)],
        out_specs=[pl.BlockSpec((m, MEM_WIDTH), const)] * 2,
        out_shape=[jax.ShapeDtypeStruct((m, MEM_WIDTH), BF16)] * 2,
        compiler_params=_cparams("arbitrary"),
        name="memkv",
    )(mem, gain, w_mem_kv, k_gain)


def _mix_kernel(x_ref, g_ref, w_ref, qg_ref, kg_ref, cw_ref, cos_ref, sin_ref, mqg_ref,
                mk_ref, mv_ref, qT_ref, k_ref, vT_ref, ycm_ref, bound_ref,
                kmean_ref, carry_ref, kmax_ref, *, tm):
    i = pl.program_id(0)
    blocks_per_tile = tm // MOBA_BLOCK
    n_blk = kmean_ref.shape[1]

    @pl.when(i == 0)
    def _():
        kmean_ref[...] = jnp.zeros_like(kmean_ref)
        carry_ref[...] = jnp.zeros_like(carry_ref)
        kmax_ref[...] = jnp.zeros_like(kmax_ref)

    h = _rms(x_ref[...], g_ref[...]).astype(BF16)
    cos = cos_ref[...]
    sin = sin_ref[...]

    def rope(t):
        return t * cos + pltpu.roll(t, HEAD_DIM // 2, 1) * sin

    c0 = 3 * ATTN_WIDTH
    slot = lax.broadcasted_iota(jnp.int32, (tm, HEAD_DIM), 1)
    row_blk = (i * tm + lax.broadcasted_iota(jnp.int32, (tm, HEAD_DIM), 0)) // MOBA_BLOCK
    one_hot = (slot == row_blk).astype(BF16)
    q_blk = (i * tm + lax.broadcasted_iota(jnp.int32, (1, tm), 1)) // MOBA_BLOCK
    blk_id = lax.broadcasted_iota(jnp.int32, (n_blk, tm), 0)
    past = blk_id < q_blk

    def project(col, width=2 * HEAD_DIM):
        return jnp.dot(h, w_ref[:, col:col + width], preferred_element_type=F32)

    def key_head(pk_h, hh):
        kr = rope(_rms(pk_h, kg_ref[...]))
        kb = kr.astype(BF16)
        k_ref[hh, :, :HEAD_DIM] = kb
        k_ref[hh, :, HEAD_DIM:] = one_hot
        kn2 = jnp.max(jnp.sum(jnp.square(kb.astype(F32)), axis=1, keepdims=True), axis=0, keepdims=True)
        kmax_ref[hh:hh + 1, :] = jnp.maximum(kmax_ref[hh:hh + 1, :], kn2)
        for b in range(blocks_per_tile):
            blk = kr[b * MOBA_BLOCK:(b + 1) * MOBA_BLOCK]
            kmean_ref[hh, pl.ds(i * blocks_per_tile + b, 1), :] = jnp.mean(blk, axis=0, keepdims=True)

    def memory_head(pm_h, hh):
        sl = slice(hh * HEAD_DIM, (hh + 1) * HEAD_DIM)
        mq = (_rms(pm_h, mqg_ref[...]) * HEAD_DIM ** -0.5).astype(BF16)
        sc = lax.dot_general(mq, mk_ref[:, sl], (((1,), (1,)), ((), ())), preferred_element_type=F32)
        p = jnp.exp(sc - jnp.max(sc, axis=-1, keepdims=True))
        o = jnp.dot(p.astype(BF16), mv_ref[:, sl], preferred_element_type=F32)
        o = o / jnp.sum(p, axis=-1, keepdims=True)
        ycm_ref[:, CONV_CH + hh * HEAD_DIM:CONV_CH + (hh + 1) * HEAD_DIM] = o.astype(BF16)

    def query_head(pq_h, hh):
        qT = rope(_rms(pq_h, qg_ref[...])).T
        km_hi, km_lo = _split_bf16(kmean_ref[hh])
        q_hi, q_lo = _split_bf16(qT)
        gate = jnp.dot(jnp.concatenate([km_hi, km_hi, km_lo], axis=1),
                       jnp.concatenate([q_hi, q_lo, q_hi], axis=0),
                       preferred_element_type=F32)
        gate = jnp.where(past, gate, -jnp.inf)
        bias = jnp.where(blk_id == q_blk, 0.0, MASK_BIAS)
        for _ in range(MOBA_TOPK):
            best = jnp.max(gate, axis=0, keepdims=True)
            first = jnp.min(jnp.where(gate == best, blk_id, n_blk), axis=0, keepdims=True)
            first = jnp.where(best > -jnp.inf, first, -1)
            pick = blk_id == first
            bias = jnp.where(pick, 0.0, bias)
            gate = jnp.where(pick, -jnp.inf, gate)
        qs = (qT * (HEAD_DIM ** -0.5 * LOG2E)).astype(BF16)
        qT_ref[hh, :HEAD_DIM, :] = qs
        qn2 = jnp.sum(jnp.square(qs.astype(F32)), axis=0, keepdims=True)
        bound = jnp.sqrt(qn2 * kmax_ref[hh:hh + 1, 0:1]) * BOUND_SLACK
        bound_ref[hh] = jnp.broadcast_to(bound, (8, tm))
        qT_ref[hh, HEAD_DIM:HEAD_DIM + n_blk, :] = bias.astype(BF16)
        qT_ref[hh, HEAD_DIM + n_blk:, :] = jnp.zeros((AUG_DIM - HEAD_DIM - n_blk, tm), BF16)

    pk = project(ATTN_WIDTH, ATTN_WIDTH)
    pq = project(0, ATTN_WIDTH)
    for hh in range(N_ATTN_HEADS):
        key_head(pk[:, hh * HEAD_DIM:(hh + 1) * HEAD_DIM], hh)
    rest = []
    for hh in range(N_ATTN_HEADS):
        if hh % 2 == 0:
            pv2 = project(2 * ATTN_WIDTH + hh * HEAD_DIM)
            for d in range(2):
                vT_ref[hh + d, 0] = pv2[:, d * HEAD_DIM:(d + 1) * HEAD_DIM].T.astype(BF16)
        rest.append(project(c0 + hh * 2 * HEAD_DIM))
        query_head(pq[:, hh * HEAD_DIM:(hh + 1) * HEAD_DIM], hh)

    rest = jnp.concatenate(rest, axis=1)
    pc = rest[:, :3 * CONV_CH]
    pm = rest[:, 3 * CONV_CH:]

    u = pc[:, CONV_CH:2 * CONV_CH] * pc[:, 2 * CONV_CH:]
    prev1 = carry_ref[7:8, :]
    prev2 = carry_ref[6:7, :]
    row = lax.broadcasted_iota(jnp.int32, u.shape, 0)
    u1 = jnp.where(row == 0, prev1, pltpu.roll(u, 1, 0))
    u2 = jnp.where(row == 0, prev2, jnp.where(row == 1, prev1, pltpu.roll(u, 2, 0)))
    carry_ref[...] = u[tm - 8:, :]
    conv = u2 * cw_ref[0:1, :] + u1 * cw_ref[1:2, :] + u * cw_ref[2:3, :]
    ycm_ref[:, :CONV_CH] = (pc[:, :CONV_CH] * conv).astype(BF16)

    for hh in range(N_MEM_HEADS):
        memory_head(pm[:, hh * HEAD_DIM:(hh + 1) * HEAD_DIM], hh)


def _mix(x, gain, w_in, q_gain, k_gain, conv_w, cos, sin, mq_gain, mk, mv, layer, *, tm):
    s = x.shape[0]
    n_blk = s // MOBA_BLOCK
    m = mk.shape[0]
    const = lambda i: (0, 0)
    return pl.pallas_call(
        functools.partial(_mix_kernel, tm=tm),
        grid=(s // tm,),
        in_specs=[pl.BlockSpec((tm, D_MODEL), lambda i: (i, 0)),
                  _resident((1, D_MODEL), const),
                  _resident((None, D_MODEL, IN_WIDTH), lambda i: (layer, 0, 0)),
                  _resident((1, HEAD_DIM), const),
                  _resident((1, HEAD_DIM), const),
                  _resident((CONV_K, CONV_CH), const),
                  pl.BlockSpec((tm, HEAD_DIM), lambda i: (i, 0)),
                  pl.BlockSpec((tm, HEAD_DIM), lambda i: (i, 0)),
                  _resident((1, HEAD_DIM), const),
                  _resident((m, MEM_WIDTH), const),
                  _resident((m, MEM_WIDTH), const)],
        out_specs=[pl.BlockSpec((N_ATTN_HEADS, AUG_DIM, tm), lambda i: (0, 0, i)),
                   pl.BlockSpec((N_ATTN_HEADS, tm, AUG_DIM), lambda i: (0, i, 0)),
                   pl.BlockSpec((N_ATTN_HEADS, 1, HEAD_DIM, tm), lambda i: (0, i, 0, 0)),
                   pl.BlockSpec((tm, CONV_CH + MEM_WIDTH), lambda i: (i, 0)),
                   pl.BlockSpec((N_ATTN_HEADS, 8, tm), lambda i: (0, 0, i))],
        out_shape=[jax.ShapeDtypeStruct((N_ATTN_HEADS, AUG_DIM, s), BF16),
                   jax.ShapeDtypeStruct((N_ATTN_HEADS, s, AUG_DIM), BF16),
                   jax.ShapeDtypeStruct((N_ATTN_HEADS, s // tm, HEAD_DIM, tm), BF16),
                   jax.ShapeDtypeStruct((s, CONV_CH + MEM_WIDTH), BF16),
                   jax.ShapeDtypeStruct((N_ATTN_HEADS, 8, s), F32)],
        scratch_shapes=[pltpu.VMEM((N_ATTN_HEADS, n_blk, HEAD_DIM), F32),
                        pltpu.VMEM((8, CONV_CH), F32),
                        pltpu.VMEM((N_ATTN_HEADS, HEAD_DIM), F32)],
        compiler_params=_cparams("arbitrary"),
        name="mix_proj",
    )(x, gain, w_in, q_gain, k_gain, conv_w, cos, sin, mq_gain, mk, mv)


Q_TILE = 2048
KEY_TILE = 512
M_INIT = 2 * MASK_BIAS
SAFE_LOGIT_BOUND = 40.0
BOUND_SLACK = 1.0 + 2.0 ** -7


def _causal_fix(s, first_lane_block):
    rows = []
    for b in range(s.shape[0] // MOBA_BLOCK):
        sb = s[b * MOBA_BLOCK:(b + 1) * MOBA_BLOCK]
        lo = (first_lane_block + b) * MOBA_BLOCK
        hi = lo + MOBA_BLOCK
        own = sb[:, lo:hi]
        key_pos = lax.broadcasted_iota(jnp.int32, own.shape, 0)
        qry_pos = lax.broadcasted_iota(jnp.int32, own.shape, 1)
        own = jnp.where(key_pos <= qry_pos, own, MASK_BIAS)
        pieces = ([sb[:, :lo]] if lo else []) + [own] + ([sb[:, hi:]] if hi < s.shape[1] else [])
        rows.append(jnp.concatenate(pieces, axis=1))
    return jnp.concatenate(rows, axis=0)


def _attn_kernel(qT_ref, k_ref, vT_ref, bound_ref, o_ref, acc_ref, l_ref, *, q_tile, key_tile):
    t = pl.program_id(1)
    qT = qT_ref[0]
    acc_ref[...] = jnp.zeros_like(acc_ref)
    l_ref[...] = jnp.zeros_like(l_ref)
    steps_per_tile = q_tile // key_tile
    bound = bound_ref[0, 0:1, :]
    bounded = jnp.max(bound) < SAFE_LOGIT_BOUND

    def scores(g, lane0, diagonal):
        rows = pl.ds(pl.multiple_of(g * key_tile, key_tile), key_tile)
        s = jnp.dot(k_ref[0, rows, :], qT[:, lane0:], preferred_element_type=F32)
        return _causal_fix(s, 0) if diagonal else s

    def update(g, s, m, lane0):
        m_old = m[:, lane0:]
        m_new = jnp.maximum(m_old, jnp.max(s, axis=0, keepdims=True))
        p = jnp.exp2(s - m_new)
        alpha = jnp.exp2(m_old - m_new)
        l_ref[:, lane0:] = l_ref[:, lane0:] * alpha + jnp.sum(p, axis=0, keepdims=True)
        acc_ref[:, lane0:] = (acc_ref[:, lane0:] * alpha
                              + jnp.dot(vT_ref[0, g], p.astype(BF16), preferred_element_type=F32))
        return jnp.concatenate([m[:, :lane0], m_new], axis=1) if lane0 else m_new

    def key_tiles(first, m, diagonal):
        tiles = [(first + d, d * key_tile if diagonal else 0) for d in range(steps_per_tile)]
        ss = [scores(g, lane0, diagonal) for g, lane0 in tiles]
        for (g, lane0), s in zip(tiles, ss):
            m = update(g, s, m, lane0)
        return m

    def bounded_tile(g, lane0, diagonal):
        p = jnp.exp2(scores(g, lane0, diagonal) - bound[:, lane0:])
        l_ref[:, lane0:] += jnp.sum(p, axis=0, keepdims=True)
        acc_ref[:, lane0:] += jnp.dot(vT_ref[0, g], p.astype(BF16), preferred_element_type=F32)

    def bounded_past(u, carry):
        for d in range(steps_per_tile):
            bounded_tile(u * steps_per_tile + d, 0, False)
        return carry

    @pl.when(bounded)
    def _():
        lax.fori_loop(0, t, bounded_past, 0)
        for d in range(steps_per_tile):
            bounded_tile(t * steps_per_tile + d, d * key_tile, True)

    @pl.when(jnp.logical_not(bounded))
    def _():
        m = jnp.full((1, q_tile), M_INIT, F32)
        m = lax.fori_loop(0, t, lambda u, m: key_tiles(u * steps_per_tile, m, False), m)
        key_tiles(t * steps_per_tile, m, True)

    out = acc_ref[...] / l_ref[...]
    o_ref[...] = out.T.astype(o_ref.dtype)


def _attention(qT, k, vT, bound):
    s = k.shape[1]
    key_tile = vT.shape[3]
    q_tile = min(Q_TILE, s)
    return pl.pallas_call(
        functools.partial(_attn_kernel, q_tile=q_tile, key_tile=key_tile),
        grid=(N_ATTN_HEADS, s // q_tile),
        in_specs=[pl.BlockSpec((1, AUG_DIM, q_tile), lambda h, t: (h, 0, t)),
                  pl.BlockSpec((1, s, AUG_DIM), lambda h, t: (h, 0, 0)),
                  pl.BlockSpec((1, s // key_tile, HEAD_DIM, key_tile), lambda h, t: (h, 0, 0, 0)),
                  pl.BlockSpec((1, 8, q_tile), lambda h, t: (h, 0, t))],
        out_specs=pl.BlockSpec((q_tile, HEAD_DIM), lambda h, t: (t, h)),
        out_shape=jax.ShapeDtypeStruct((s, ATTN_WIDTH), BF16),
        scratch_shapes=[pltpu.VMEM((HEAD_DIM, q_tile), F32), pltpu.VMEM((1, q_tile), F32)],
        compiler_params=_cparams("parallel", "arbitrary"),
        name="moba_attn",
    )(qT, k, vT, bound)


def _outproj_kernel(x_ref, ya_ref, ycm_ref, w_ref, o_ref):
    o_ref[...] = (x_ref[...]
                  + jnp.dot(ya_ref[...], w_ref[:ATTN_WIDTH], preferred_element_type=F32)
                  + jnp.dot(ycm_ref[...], w_ref[ATTN_WIDTH:], preferred_element_type=F32))


def _outproj(x, y_attn, y_cm, w_out, layer, *, tm):
    s = x.shape[0]
    return pl.pallas_call(
        _outproj_kernel,
        grid=(s // tm,),
        in_specs=[pl.BlockSpec((tm, D_MODEL), lambda i: (i, 0)),
                  pl.BlockSpec((tm, ATTN_WIDTH), lambda i: (i, 0)),
                  pl.BlockSpec((tm, CONV_CH + MEM_WIDTH), lambda i: (i, 0)),
                  _resident((None, D_MODEL, D_MODEL), lambda i: (layer, 0, 0))],
        out_specs=pl.BlockSpec((tm, D_MODEL), lambda i: (i, 0)),
        out_shape=jax.ShapeDtypeStruct((s, D_MODEL), F32),
        compiler_params=_cparams("parallel"),
        name="out_proj",
    )(x, y_attn, y_cm, w_out)


def kernel(x, mem, positions, ffn1_norm, ffn1_w_gate_up, ffn1_w_down, mix_norm, w_in, q_norm, k_norm,
           conv_w, mem_norm, w_mem_kv, mq_norm, mk_norm, w_out, ffn2_norm, ffn2_w_gate_up, ffn2_w_down):
    b, s, _ = x.shape
    assert b == 1 and s % MOBA_BLOCK == 0 and s // MOBA_BLOCK <= HEAD_DIM // 2
    depth = w_in.shape[0]
    tm = min(s, KEY_TILE)
    ffn = functools.partial(_ffn, tm=min(s, FFN_ROW_TILE), tf=512)

    ffn_weights = [(ffn1_w_gate_up, ffn1_w_down), (ffn2_w_gate_up, ffn2_w_down)]
    w_gu, w_d = (w[0].astype(BF16) for w in ffn_weights[0])
    ffn_gains = [ffn1_norm, ffn2_norm]
    w_in_b, w_out_b, w_mkv = (w.astype(BF16) for w in (w_in, w_out, w_mem_kv))
    cos, sin = _rope_tables(positions[0])
    xs = x[0]
    mem2 = mem[0]

    def ffn_call(xs, which, l, w_gu, w_d):
        nxt = 2 * l + which + 1
        next_weights = None if nxt == 2 * depth else ffn_weights[nxt % 2] + (nxt // 2,)
        out = ffn(xs, ffn_gains[which][l][None], w_gu, w_d, next_weights)
        return tuple(out) if next_weights else (out[0], None, None)

    for l in range(depth):
        xs, w_gu, w_d = ffn_call(xs, 0, l, w_gu, w_d)
        mk, mv = _memkv(mem2, mem_norm[l][None], w_mkv, mk_norm[l][None], l)
        qT, k, vT, y_cm, bound = _mix(xs, mix_norm[l][None], w_in_b, q_norm[l][None],
                                      k_norm[l][None], conv_w[l], cos, sin, mq_norm[l][None],
                                      mk, mv, l, tm=tm)
        y_attn = _attention(qT, k, vT, bound)
        xs = _outproj(xs, y_attn, y_cm, w_out_b, l, tm=tm)
        xs, w_gu, w_d = ffn_call(xs, 1, l, w_gu, w_d)
    return xs[None]
```

```python
import functools
import math

import jax
import jax.numpy as jnp
from jax import lax
from jax.experimental import pallas as pl
from jax.experimental.pallas import tpu as pltpu

F32 = jnp.float32
BF16 = jnp.bfloat16

D_MODEL = 2048
HEAD_DIM = 128
ATTN_WIDTH = 1024
N_ATTN_HEADS = 8
CONV_CH = 512
CONV_K = 3
MEM_WIDTH = 512
N_MEM_HEADS = 4
IN_WIDTH = 3 * ATTN_WIDTH + 3 * CONV_CH + MEM_WIDTH
D_FF = 5632
FFN_RES = 0.5
MOBA_BLOCK = 256
MOBA_TOPK = 3
ROPE_THETA = 10000.0
RMS_EPS = 1e-6

SUBLANES = 8
V7X_VMEM_BYTES = 64 * 1024 * 1024
VMEM_LIMIT = V7X_VMEM_BYTES - 6 * 1024 * 1024

MASK_BIAS = -1e30
AUG_DIM = 2 * HEAD_DIM
LOG2E = math.log2(math.e)


def _rms(x, gain):
    return x * lax.rsqrt(jnp.mean(x * x, axis=-1, keepdims=True) + RMS_EPS) * gain


def _split_bf16(a):
    hi = a.astype(BF16)
    return hi, (a - hi.astype(F32)).astype(BF16)


def _cparams(*sem):
    return pltpu.CompilerParams(dimension_semantics=sem, vmem_limit_bytes=VMEM_LIMIT)


def _resident(shape, index_map):
    return pl.BlockSpec(shape, index_map, pipeline_mode=pl.Buffered(1))


ROPE_ROW_TILE = 2048


def _rope_kernel(pos_ref, inv_ref, cos_ref, sin_ref):
    ang = pos_ref[...].astype(F32) * inv_ref[...]
    lane = lax.broadcasted_iota(jnp.int32, ang.shape, 1)
    s = jnp.sin(ang)
    cos_ref[...] = jnp.cos(ang)
    sin_ref[...] = jnp.where(lane < HEAD_DIM // 2, -s, s)


def _rope_tables(positions):
    s = positions.shape[0]
    t = min(s, ROPE_ROW_TILE)
    inv = ROPE_THETA ** (-jnp.arange(0, HEAD_DIM, 2, dtype=F32) / HEAD_DIM)
    inv = jnp.concatenate([inv, inv])[None, :]
    return pl.pallas_call(
        _rope_kernel,
        grid=(s // t,),
        in_specs=[pl.BlockSpec((t, 1), lambda i: (i, 0)),
                  pl.BlockSpec((1, HEAD_DIM), lambda i: (0, 0))],
        out_specs=[pl.BlockSpec((t, HEAD_DIM), lambda i: (i, 0))] * 2,
        out_shape=[jax.ShapeDtypeStruct((s, HEAD_DIM), F32)] * 2,
        compiler_params=_cparams("parallel"),
        name="rope_tables",
    )(positions[:, None], inv)


FFN_ROW_TILE = 1024
FFN_FF_TILE = 512


def _ffn_kernel(*refs, convert_next):
    if convert_next:
        x_ref, g_ref, wg_ref, wu_ref, wd_ref, ngu_ref, nd_ref, o_ref, ngu_out, nd_out, h_ref = refs
    else:
        x_ref, g_ref, wg_ref, wu_ref, wd_ref, o_ref, h_ref = refs
    j = pl.program_id(1)

    def half_step(h):
        gate = jnp.dot(h, wg_ref[...], preferred_element_type=F32)
        up = jnp.dot(h, wu_ref[...], preferred_element_type=F32)
        act = (gate * jax.nn.sigmoid(gate) * up * FFN_RES).astype(BF16)
        if convert_next:
            ngu_out[...] = ngu_ref[...].astype(BF16)
            nd_out[...] = nd_ref[...].astype(BF16)
        return jnp.dot(act, wd_ref[...], preferred_element_type=F32)

    @pl.when(j == 0)
    def _():
        x = x_ref[...]
        h = _rms(x, g_ref[...]).astype(BF16)
        h_ref[...] = h
        o_ref[...] = x + half_step(h)

    @pl.when(j > 0)
    def _():
        o_ref[...] += half_step(h_ref[...])


def _ffn(x, gain, w_gate_up, w_down, next_weights=None, *, tm, tf):
    s = x.shape[0]
    n_i, n_f = s // tm, D_FF // tf
    in_specs = [pl.BlockSpec((tm, D_MODEL), lambda i, j: (i, 0)),
                pl.BlockSpec((1, D_MODEL), lambda i, j: (0, 0)),
                pl.BlockSpec((D_MODEL, tf), lambda i, j: (0, j)),
                pl.BlockSpec((D_MODEL, tf), lambda i, j: (0, j + n_f)),
                pl.BlockSpec((tf, D_MODEL), lambda i, j: (j, 0))]
    out_specs = [pl.BlockSpec((tm, D_MODEL), lambda i, j: (i, 0))]
    out_shape = [jax.ShapeDtypeStruct((s, D_MODEL), F32)]
    operands = [x, gain, w_gate_up, w_gate_up, w_down]
    if next_weights is not None:
        gu32, d32, layer = next_weights
        gu_blk = (D_MODEL // n_i, 2 * D_FF // n_f)
        d_blk = (D_FF // (n_i * n_f), D_MODEL)
        in_specs += [pl.BlockSpec((None,) + gu_blk, lambda i, j: (layer, i, j)),
                     pl.BlockSpec((None,) + d_blk, lambda i, j: (layer, i * n_f + j, 0))]
        out_specs += [pl.BlockSpec(gu_blk, lambda i, j: (i, j)),
                      pl.BlockSpec(d_blk, lambda i, j: (i * n_f + j, 0))]
        out_shape += [jax.ShapeDtypeStruct(gu32.shape[1:], BF16), jax.ShapeDtypeStruct(d32.shape[1:], BF16)]
        operands += [gu32, d32]
    return pl.pallas_call(
        functools.partial(_ffn_kernel, convert_next=next_weights is not None),
        grid=(n_i, n_f),
        in_specs=in_specs,
        out_specs=out_specs,
        out_shape=out_shape,
        scratch_shapes=[pltpu.VMEM((tm, D_MODEL), BF16)],
        compiler_params=_cparams("parallel", "arbitrary"),
        name="ffn",
    )(*operands)


def _memkv_kernel(mem_ref, g_ref, w_ref, kg_ref, mk_ref, mv_ref):
    h = _rms(mem_ref[...], g_ref[...]).astype(BF16)
    kv = jnp.dot(h, w_ref[...], preferred_element_type=F32)
    for hh in range(N_MEM_HEADS):
        sl = slice(hh * HEAD_DIM, (hh + 1) * HEAD_DIM)
        mk_ref[:, sl] = _rms(kv[:, sl], kg_ref[...]).astype(BF16)
    mv_ref[...] = kv[:, MEM_WIDTH:].astype(BF16)


def _memkv(mem, gain, w_mem_kv, k_gain, layer):
    m = mem.shape[0]
    const = lambda i: (0, 0)
    return pl.pallas_call(
        _memkv_kernel,
        grid=(1,),
        in_specs=[pl.BlockSpec((m, D_MODEL), const),
                  pl.BlockSpec((1, D_MODEL), const),
                  pl.BlockSpec((None, D_MODEL, 2 * MEM_WIDTH), lambda i: (layer, 0, 0)),
                  pl.BlockSpec((1, HEAD_DIM), const)],
        out_specs=[pl.BlockSpec((m, MEM_WIDTH), const)] * 2,
        out_shape=[jax.ShapeDtypeStruct((m, MEM_WIDTH), BF16)] * 2,
        compiler_params=_cparams("arbitrary"),
        name="memkv",
    )(mem, gain, w_mem_kv, k_gain)


def _mix_kernel(x_ref, g_ref, w_ref, qg_ref, kg_ref, cw_ref, cos_ref, sin_ref, mqg_ref,
                mk_ref, mv_ref, qT_ref, k_ref, vT_ref, ycm_ref, bound_ref,
                kmean_ref, carry_ref, kmax_ref, *, tm):
    i = pl.program_id(0)
    blocks_per_tile = tm // MOBA_BLOCK
    n_blk = kmean_ref.shape[1]

    @pl.when(i == 0)
    def _():
        kmean_ref[...] = jnp.zeros_like(kmean_ref)
        carry_ref[...] = jnp.zeros_like(carry_ref)
        kmax_ref[...] = jnp.zeros_like(kmax_ref)

    h = _rms(x_ref[...], g_ref[...]).astype(BF16)
    cos = cos_ref[...]
    sin = sin_ref[...]

    def rope(t):
        return t * cos + pltpu.roll(t, HEAD_DIM // 2, 1) * sin

    c0 = 3 * ATTN_WIDTH
    slot = lax.broadcasted_iota(jnp.int32, (tm, HEAD_DIM), 1)
    row_blk = (i * tm + lax.broadcasted_iota(jnp.int32, (tm, HEAD_DIM), 0)) // MOBA_BLOCK
    one_hot = (slot == row_blk).astype(BF16)
    q_blk = (i * tm + lax.broadcasted_iota(jnp.int32, (1, tm), 1)) // MOBA_BLOCK
    blk_id = lax.broadcasted_iota(jnp.int32, (n_blk, tm), 0)
    past = blk_id < q_blk

    def project(col, width=2 * HEAD_DIM):
        return jnp.dot(h, w_ref[:, col:col + width], preferred_element_type=F32)

    def key_head(pk_h, hh):
        kr = rope(_rms(pk_h, kg_ref[...]))
        kb = kr.astype(BF16)
        k_ref[hh, :, :HEAD_DIM] = kb
        k_ref[hh, :, HEAD_DIM:] = one_hot
        kn2 = jnp.max(jnp.sum(jnp.square(kb.astype(F32)), axis=1, keepdims=True), axis=0, keepdims=True)
        kmax_ref[hh:hh + 1, :] = jnp.maximum(kmax_ref[hh:hh + 1, :], kn2)
        for b in range(blocks_per_tile):
            blk = kr[b * MOBA_BLOCK:(b + 1) * MOBA_BLOCK]
            kmean_ref[hh, pl.ds(i * blocks_per_tile + b, 1), :] = jnp.mean(blk, axis=0, keepdims=True)

    def memory_head(pm_h, hh):
        sl = slice(hh * HEAD_DIM, (hh + 1) * HEAD_DIM)
        mq = (_rms(pm_h, mqg_ref[...]) * HEAD_DIM ** -0.5).astype(BF16)
        sc = lax.dot_general(mq, mk_ref[:, sl], (((1,), (1,)), ((), ())), preferred_element_type=F32)
        p = jnp.exp(sc - jnp.max(sc, axis=-1, keepdims=True))
        o = jnp.dot(p.astype(BF16), mv_ref[:, sl], preferred_element_type=F32)
        o = o / jnp.sum(p, axis=-1, keepdims=True)
        ycm_ref[:, CONV_CH + hh * HEAD_DIM:CONV_CH + (hh + 1) * HEAD_DIM] = o.astype(BF16)

    def query_head(pq_h, hh):
        qT = rope(_rms(pq_h, qg_ref[...])).T
        km_hi, km_lo = _split_bf16(kmean_ref[hh])
        q_hi, q_lo = _split_bf16(qT)
        gate = jnp.dot(jnp.concatenate([km_hi, km_hi, km_lo], axis=1),
                       jnp.concatenate([q_hi, q_lo, q_hi], axis=0),
                       preferred_element_type=F32)
        gate = jnp.where(past, gate, -jnp.inf)
        bias = jnp.where(blk_id == q_blk, 0.0, MASK_BIAS)
        for _ in range(MOBA_TOPK):
            best = jnp.max(gate, axis=0, keepdims=True)
            first = jnp.min(jnp.where(gate == best, blk_id, n_blk), axis=0, keepdims=True)
            first = jnp.where(best > -jnp.inf, first, -1)
            pick = blk_id == first
            bias = jnp.where(pick, 0.0, bias)
            gate = jnp.where(pick, -jnp.inf, gate)
        qs = (qT * (HEAD_DIM ** -0.5 * LOG2E)).astype(BF16)
        qT_ref[hh, :HEAD_DIM, :] = qs
        qn2 = jnp.sum(jnp.square(qs.astype(F32)), axis=0, keepdims=True)
        bound = jnp.sqrt(qn2 * kmax_ref[hh:hh + 1, 0:1]) * BOUND_SLACK
        bound_ref[hh] = jnp.broadcast_to(bound, (SUBLANES, tm))
        qT_ref[hh, HEAD_DIM:HEAD_DIM + n_blk, :] = bias.astype(BF16)
        qT_ref[hh, HEAD_DIM + n_blk:, :] = jnp.zeros((AUG_DIM - HEAD_DIM - n_blk, tm), BF16)

    pk = project(ATTN_WIDTH, ATTN_WIDTH)
    pq = project(0, ATTN_WIDTH)
    for hh in range(N_ATTN_HEADS):
        key_head(pk[:, hh * HEAD_DIM:(hh + 1) * HEAD_DIM], hh)
    rest = []
    for hh in range(N_ATTN_HEADS):
        if hh % 2 == 0:
            pv2 = project(2 * ATTN_WIDTH + hh * HEAD_DIM)
            for d in range(2):
                vT_ref[hh + d, 0] = pv2[:, d * HEAD_DIM:(d + 1) * HEAD_DIM].T.astype(BF16)
        rest.append(project(c0 + hh * 2 * HEAD_DIM))
        query_head(pq[:, hh * HEAD_DIM:(hh + 1) * HEAD_DIM], hh)

    rest = jnp.concatenate(rest, axis=1)
    pc = rest[:, :3 * CONV_CH]
    pm = rest[:, 3 * CONV_CH:]

    u = pc[:, CONV_CH:2 * CONV_CH] * pc[:, 2 * CONV_CH:]
    prev1 = carry_ref[SUBLANES - 1:SUBLANES, :]
    prev2 = carry_ref[SUBLANES - 2:SUBLANES - 1, :]
    row = lax.broadcasted_iota(jnp.int32, u.shape, 0)
    u1 = jnp.where(row == 0, prev1, pltpu.roll(u, 1, 0))
    u2 = jnp.where(row == 0, prev2, jnp.where(row == 1, prev1, pltpu.roll(u, 2, 0)))
    carry_ref[...] = u[tm - SUBLANES:, :]
    conv = u2 * cw_ref[0:1, :] + u1 * cw_ref[1:2, :] + u * cw_ref[2:3, :]
    ycm_ref[:, :CONV_CH] = (pc[:, :CONV_CH] * conv).astype(BF16)

    for hh in range(N_MEM_HEADS):
        memory_head(pm[:, hh * HEAD_DIM:(hh + 1) * HEAD_DIM], hh)


def _mix(x, gain, w_in, q_gain, k_gain, conv_w, cos, sin, mq_gain, mk, mv, layer, *, tm):
    s = x.shape[0]
    n_blk = s // MOBA_BLOCK
    m = mk.shape[0]
    const = lambda i: (0, 0)
    return pl.pallas_call(
        functools.partial(_mix_kernel, tm=tm),
        grid=(s // tm,),
        in_specs=[pl.BlockSpec((tm, D_MODEL), lambda i: (i, 0)),
                  _resident((1, D_MODEL), const),
                  _resident((None, D_MODEL, IN_WIDTH), lambda i: (layer, 0, 0)),
                  _resident((1, HEAD_DIM), const),
                  _resident((1, HEAD_DIM), const),
                  _resident((CONV_K, CONV_CH), const),
                  pl.BlockSpec((tm, HEAD_DIM), lambda i: (i, 0)),
                  pl.BlockSpec((tm, HEAD_DIM), lambda i: (i, 0)),
                  _resident((1, HEAD_DIM), const),
                  _resident((m, MEM_WIDTH), const),
                  _resident((m, MEM_WIDTH), const)],
        out_specs=[pl.BlockSpec((N_ATTN_HEADS, AUG_DIM, tm), lambda i: (0, 0, i)),
                   pl.BlockSpec((N_ATTN_HEADS, tm, AUG_DIM), lambda i: (0, i, 0)),
                   pl.BlockSpec((N_ATTN_HEADS, 1, HEAD_DIM, tm), lambda i: (0, i, 0, 0)),
                   pl.BlockSpec((tm, CONV_CH + MEM_WIDTH), lambda i: (i, 0)),
                   pl.BlockSpec((N_ATTN_HEADS, SUBLANES, tm), lambda i: (0, 0, i))],
        out_shape=[jax.ShapeDtypeStruct((N_ATTN_HEADS, AUG_DIM, s), BF16),
                   jax.ShapeDtypeStruct((N_ATTN_HEADS, s, AUG_DIM), BF16),
                   jax.ShapeDtypeStruct((N_ATTN_HEADS, s // tm, HEAD_DIM, tm), BF16),
                   jax.ShapeDtypeStruct((s, CONV_CH + MEM_WIDTH), BF16),
                   jax.ShapeDtypeStruct((N_ATTN_HEADS, SUBLANES, s), F32)],
        scratch_shapes=[pltpu.VMEM((N_ATTN_HEADS, n_blk, HEAD_DIM), F32),
                        pltpu.VMEM((SUBLANES, CONV_CH), F32),
                        pltpu.VMEM((N_ATTN_HEADS, HEAD_DIM), F32)],
        compiler_params=_cparams("arbitrary"),
        name="mix_proj",
    )(x, gain, w_in, q_gain, k_gain, conv_w, cos, sin, mq_gain, mk, mv)


Q_TILE = 2048
KEY_TILE = 512
M_INIT = 2 * MASK_BIAS
SAFE_LOGIT_BOUND = 40.0
BOUND_SLACK = 1.0 + 2.0 ** -7


def _causal_fix(s, first_lane_block):
    rows = []
    for b in range(s.shape[0] // MOBA_BLOCK):
        sb = s[b * MOBA_BLOCK:(b + 1) * MOBA_BLOCK]
        lo = (first_lane_block + b) * MOBA_BLOCK
        hi = lo + MOBA_BLOCK
        own = sb[:, lo:hi]
        key_pos = lax.broadcasted_iota(jnp.int32, own.shape, 0)
        qry_pos = lax.broadcasted_iota(jnp.int32, own.shape, 1)
        own = jnp.where(key_pos <= qry_pos, own, MASK_BIAS)
        pieces = ([sb[:, :lo]] if lo else []) + [own] + ([sb[:, hi:]] if hi < s.shape[1] else [])
        rows.append(jnp.concatenate(pieces, axis=1))
    return jnp.concatenate(rows, axis=0)


def _attn_kernel(qT_ref, k_ref, vT_ref, bound_ref, o_ref, acc_ref, l_ref, *, q_tile, key_tile):
    t = pl.program_id(1)
    qT = qT_ref[0]
    acc_ref[...] = jnp.zeros_like(acc_ref)
    l_ref[...] = jnp.zeros_like(l_ref)
    steps_per_tile = q_tile // key_tile
    bound = bound_ref[0, 0:1, :]
    bounded = jnp.max(bound) < SAFE_LOGIT_BOUND

    def scores(g, lane0, diagonal):
        rows = pl.ds(pl.multiple_of(g * key_tile, key_tile), key_tile)
        s = jnp.dot(k_ref[0, rows, :], qT[:, lane0:], preferred_element_type=F32)
        return _causal_fix(s, 0) if diagonal else s

    def update(g, s, m, lane0):
        m_old = m[:, lane0:]
        m_new = jnp.maximum(m_old, jnp.max(s, axis=0, keepdims=True))
        p = jnp.exp2(s - m_new)
        alpha = jnp.exp2(m_old - m_new)
        l_ref[:, lane0:] = l_ref[:, lane0:] * alpha + jnp.sum(p, axis=0, keepdims=True)
        acc_ref[:, lane0:] = (acc_ref[:, lane0:] * alpha
                              + jnp.dot(vT_ref[0, g], p.astype(BF16), preferred_element_type=F32))
        return jnp.concatenate([m[:, :lane0], m_new], axis=1) if lane0 else m_new

    def key_tiles(first, m, diagonal):
        tiles = [(first + d, d * key_tile if diagonal else 0) for d in range(steps_per_tile)]
        ss = [scores(g, lane0, diagonal) for g, lane0 in tiles]
        for (g, lane0), s in zip(tiles, ss):
            m = update(g, s, m, lane0)
        return m

    def bounded_tile(g, lane0, diagonal):
        p = jnp.exp2(scores(g, lane0, diagonal) - bound[:, lane0:])
        l_ref[:, lane0:] += jnp.sum(p, axis=0, keepdims=True)
        acc_ref[:, lane0:] += jnp.dot(vT_ref[0, g], p.astype(BF16), preferred_element_type=F32)

    def bounded_past(u, carry):
        for d in range(steps_per_tile):
            bounded_tile(u * steps_per_tile + d, 0, False)
        return carry

    @pl.when(bounded)
    def _():
        lax.fori_loop(0, t, bounded_past, 0)
        for d in range(steps_per_tile):
            bounded_tile(t * steps_per_tile + d, d * key_tile, True)

    @pl.when(jnp.logical_not(bounded))
    def _():
        m = jnp.full((1, q_tile), M_INIT, F32)
        m = lax.fori_loop(0, t, lambda u, m: key_tiles(u * steps_per_tile, m, False), m)
        key_tiles(t * steps_per_tile, m, True)

    out = acc_ref[...] / l_ref[...]
    o_ref[...] = out.T.astype(o_ref.dtype)


def _attention(qT, k, vT, bound):
    s = k.shape[1]
    key_tile = vT.shape[3]
    q_tile = min(Q_TILE, s)
    return pl.pallas_call(
        functools.partial(_attn_kernel, q_tile=q_tile, key_tile=key_tile),
        grid=(N_ATTN_HEADS, s // q_tile),
        in_specs=[pl.BlockSpec((1, AUG_DIM, q_tile), lambda h, t: (h, 0, t)),
                  pl.BlockSpec((1, s, AUG_DIM), lambda h, t: (h, 0, 0)),
                  pl.BlockSpec((1, s // key_tile, HEAD_DIM, key_tile), lambda h, t: (h, 0, 0, 0)),
                  pl.BlockSpec((1, SUBLANES, q_tile), lambda h, t: (h, 0, t))],
        out_specs=pl.BlockSpec((q_tile, HEAD_DIM), lambda h, t: (t, h)),
        out_shape=jax.ShapeDtypeStruct((s, ATTN_WIDTH), BF16),
        scratch_shapes=[pltpu.VMEM((HEAD_DIM, q_tile), F32), pltpu.VMEM((1, q_tile), F32)],
        compiler_params=_cparams("parallel", "arbitrary"),
        name="moba_attn",
    )(qT, k, vT, bound)


def _outproj_kernel(x_ref, ya_ref, ycm_ref, w_ref, o_ref):
    o_ref[...] = (x_ref[...]
                  + jnp.dot(ya_ref[...], w_ref[:ATTN_WIDTH], preferred_element_type=F32)
                  + jnp.dot(ycm_ref[...], w_ref[ATTN_WIDTH:], preferred_element_type=F32))


def _outproj(x, y_attn, y_cm, w_out, layer, *, tm):
    s = x.shape[0]
    return pl.pallas_call(
        _outproj_kernel,
        grid=(s // tm,),
        in_specs=[pl.BlockSpec((tm, D_MODEL), lambda i: (i, 0)),
                  pl.BlockSpec((tm, ATTN_WIDTH), lambda i: (i, 0)),
                  pl.BlockSpec((tm, CONV_CH + MEM_WIDTH), lambda i: (i, 0)),
                  _resident((None, D_MODEL, D_MODEL), lambda i: (layer, 0, 0))],
        out_specs=pl.BlockSpec((tm, D_MODEL), lambda i: (i, 0)),
        out_shape=jax.ShapeDtypeStruct((s, D_MODEL), F32),
        compiler_params=_cparams("parallel"),
        name="out_proj",
    )(x, y_attn, y_cm, w_out)


def kernel(x, mem, positions, ffn1_norm, ffn1_w_gate_up, ffn1_w_down, mix_norm, w_in, q_norm, k_norm,
           conv_w, mem_norm, w_mem_kv, mq_norm, mk_norm, w_out, ffn2_norm, ffn2_w_gate_up, ffn2_w_down):
    b, s, _ = x.shape
    assert b == 1 and s % KEY_TILE == 0 and s // MOBA_BLOCK <= AUG_DIM - HEAD_DIM
    depth = w_in.shape[0]
    tm = min(s, KEY_TILE)
    ffn = functools.partial(_ffn, tm=min(s, FFN_ROW_TILE), tf=FFN_FF_TILE)

    ffn_weights = [(ffn1_w_gate_up, ffn1_w_down), (ffn2_w_gate_up, ffn2_w_down)]
    w_gu, w_d = (w[0].astype(BF16) for w in ffn_weights[0])
    ffn_gains = [ffn1_norm, ffn2_norm]
    w_in_b, w_out_b, w_mkv = (w.astype(BF16) for w in (w_in, w_out, w_mem_kv))
    cos, sin = _rope_tables(positions[0])
    xs = x[0]
    mem2 = mem[0]

    def ffn_call(xs, which, l, w_gu, w_d):
        nxt = 2 * l + which + 1
        next_weights = None if nxt == 2 * depth else ffn_weights[nxt % 2] + (nxt // 2,)
        out = ffn(xs, ffn_gains[which][l][None], w_gu, w_d, next_weights)
        return tuple(out) if next_weights else (out[0], None, None)

    for l in range(depth):
        xs, w_gu, w_d = ffn_call(xs, 0, l, w_gu, w_d)
        mk, mv = _memkv(mem2, mem_norm[l][None], w_mkv, mk_norm[l][None], l)
        qT, k, vT, y_cm, bound = _mix(xs, mix_norm[l][None], w_in_b, q_norm[l][None],
                                      k_norm[l][None], conv_w[l], cos, sin, mq_norm[l][None],
                                      mk, mv, l, tm=tm)
        y_attn = _attention(qT, k, vT, bound)
        xs = _outproj(xs, y_attn, y_cm, w_out_b, l, tm=tm)
        xs, w_gu, w_d = ffn_call(xs, 1, l, w_gu, w_d)
    return xs[None]
```

```python
import functools
import math

import jax
import jax.numpy as jnp
from jax import lax
from jax.experimental import pallas as pl
from jax.experimental.pallas import tpu as pltpu

F32 = jnp.float32
BF16 = jnp.bfloat16

D_MODEL = 2048
HEAD_DIM = 128
ATTN_WIDTH = 1024
N_ATTN_HEADS = 8
CONV_CH = 512
CONV_K = 3
MEM_WIDTH = 512
N_MEM_HEADS = 4
IN_WIDTH = 3 * ATTN_WIDTH + 3 * CONV_CH + MEM_WIDTH
D_FF = 5632
FFN_RES = 0.5
MOBA_BLOCK = 256
MOBA_TOPK = 3
ROPE_THETA = 10000.0
RMS_EPS = 1e-6

V7X_VMEM_BYTES = 64 * 1024 * 1024
VMEM_LIMIT = V7X_VMEM_BYTES - 6 * 1024 * 1024
SUBLANES = 8

MASK_BIAS = -1e30
AUG_DIM = 2 * HEAD_DIM
LOG2E = math.log2(math.e)


def _rms(x, gain):
    return x * lax.rsqrt(jnp.mean(x * x, axis=-1, keepdims=True) + RMS_EPS) * gain


def _split_bf16(a):
    hi = a.astype(BF16)
    return hi, (a - hi.astype(F32)).astype(BF16)


def _cparams(*sem):
    return pltpu.CompilerParams(dimension_semantics=sem, vmem_limit_bytes=VMEM_LIMIT)


def _resident(shape, index_map):
    return pl.BlockSpec(shape, index_map, pipeline_mode=pl.Buffered(1))


ROPE_ROW_TILE = 2048


def _rope_kernel(pos_ref, inv_ref, cos_ref, sin_ref):
    ang = pos_ref[...].astype(F32) * inv_ref[...]
    lane = lax.broadcasted_iota(jnp.int32, ang.shape, 1)
    s = jnp.sin(ang)
    cos_ref[...] = jnp.cos(ang)
    sin_ref[...] = jnp.where(lane < HEAD_DIM // 2, -s, s)


def _rope_tables(positions):
    s = positions.shape[0]
    t = min(s, ROPE_ROW_TILE)
    inv = ROPE_THETA ** (-jnp.arange(0, HEAD_DIM, 2, dtype=F32) / HEAD_DIM)
    inv = jnp.concatenate([inv, inv])[None, :]
    return pl.pallas_call(
        _rope_kernel,
        grid=(s // t,),
        in_specs=[pl.BlockSpec((t, 1), lambda i: (i, 0)),
                  pl.BlockSpec((1, HEAD_DIM), lambda i: (0, 0))],
        out_specs=[pl.BlockSpec((t, HEAD_DIM), lambda i: (i, 0))] * 2,
        out_shape=[jax.ShapeDtypeStruct((s, HEAD_DIM), F32)] * 2,
        compiler_params=_cparams("parallel"),
        name="rope_tables",
    )(positions[:, None], inv)


ROW_TILE = 1024
FFN_FF_TILE = 512


def _ffn_kernel(*refs, convert_next):
    if convert_next:
        x_ref, g_ref, wg_ref, wu_ref, wd_ref, ngu_ref, nd_ref, o_ref, ngu_out, nd_out, h_ref = refs
    else:
        x_ref, g_ref, wg_ref, wu_ref, wd_ref, o_ref, h_ref = refs
    j = pl.program_id(1)

    def half_step(h):
        gate = jnp.dot(h, wg_ref[...], preferred_element_type=F32)
        up = jnp.dot(h, wu_ref[...], preferred_element_type=F32)
        act = (gate * jax.nn.sigmoid(gate) * up * FFN_RES).astype(BF16)
        if convert_next:
            ngu_out[...] = ngu_ref[...].astype(BF16)
            nd_out[...] = nd_ref[...].astype(BF16)
        return jnp.dot(act, wd_ref[...], preferred_element_type=F32)

    @pl.when(j == 0)
    def _():
        x = x_ref[...]
        h = _rms(x, g_ref[...]).astype(BF16)
        h_ref[...] = h
        o_ref[...] = x + half_step(h)

    @pl.when(j > 0)
    def _():
        o_ref[...] += half_step(h_ref[...])


def _ffn(x, gain, w_gate_up, w_down, next_weights=None, *, tm, tf):
    s = x.shape[0]
    n_i, n_f = s // tm, D_FF // tf
    in_specs = [pl.BlockSpec((tm, D_MODEL), lambda i, j: (i, 0)),
                pl.BlockSpec((1, D_MODEL), lambda i, j: (0, 0)),
                pl.BlockSpec((D_MODEL, tf), lambda i, j: (0, j)),
                pl.BlockSpec((D_MODEL, tf), lambda i, j: (0, j + n_f)),
                pl.BlockSpec((tf, D_MODEL), lambda i, j: (j, 0))]
    out_specs = [pl.BlockSpec((tm, D_MODEL), lambda i, j: (i, 0))]
    out_shape = [jax.ShapeDtypeStruct((s, D_MODEL), F32)]
    operands = [x, gain, w_gate_up, w_gate_up, w_down]
    if next_weights is not None:
        gu32, d32, layer = next_weights
        gu_blk = (D_MODEL // n_i, 2 * D_FF // n_f)
        d_blk = (D_FF // (n_i * n_f), D_MODEL)
        in_specs += [pl.BlockSpec((None,) + gu_blk, lambda i, j: (layer, i, j)),
                     pl.BlockSpec((None,) + d_blk, lambda i, j: (layer, i * n_f + j, 0))]
        out_specs += [pl.BlockSpec(gu_blk, lambda i, j: (i, j)),
                      pl.BlockSpec(d_blk, lambda i, j: (i * n_f + j, 0))]
        out_shape += [jax.ShapeDtypeStruct(gu32.shape[1:], BF16), jax.ShapeDtypeStruct(d32.shape[1:], BF16)]
        operands += [gu32, d32]
    return pl.pallas_call(
        functools.partial(_ffn_kernel, convert_next=next_weights is not None),
        grid=(n_i, n_f),
        in_specs=in_specs,
        out_specs=out_specs,
        out_shape=out_shape,
        scratch_shapes=[pltpu.VMEM((tm, D_MODEL), BF16)],
        compiler_params=_cparams("parallel", "arbitrary"),
        name="ffn",
    )(*operands)


def _memkv_kernel(mem_ref, g_ref, w_ref, kg_ref, mk_ref, mv_ref):
    h = _rms(mem_ref[...], g_ref[...]).astype(BF16)
    kv = jnp.dot(h, w_ref[...], preferred_element_type=F32)
    for hh in range(N_MEM_HEADS):
        sl = slice(hh * HEAD_DIM, (hh + 1) * HEAD_DIM)
        mk_ref[:, sl] = _rms(kv[:, sl], kg_ref[...]).astype(BF16)
    mv_ref[...] = kv[:, MEM_WIDTH:].astype(BF16)


def _memkv(mem, gain, w_mem_kv, k_gain, layer):
    m = mem.shape[0]
    const = lambda i: (0, 0)
    return pl.pallas_call(
        _memkv_kernel,
        grid=(1,),
        in_specs=[pl.BlockSpec((m, D_MODEL), const),
                  pl.BlockSpec((1, D_MODEL), const),
                  pl.BlockSpec((None, D_MODEL, 2 * MEM_WIDTH), lambda i: (layer, 0, 0)),
                  pl.BlockSpec((1, HEAD_DIM), const)],
        out_specs=[pl.BlockSpec((m, MEM_WIDTH), const)] * 2,
        out_shape=[jax.ShapeDtypeStruct((m, MEM_WIDTH), BF16)] * 2,
        compiler_params=_cparams("arbitrary"),
        name="memkv",
    )(mem, gain, w_mem_kv, k_gain)


def _mix_kernel(x_ref, g_ref, w_ref, qg_ref, kg_ref, cw_ref, cos_ref, sin_ref, mqg_ref,
                mk_ref, mv_ref, qT_ref, k_ref, vT_ref, ycm_ref, bound_ref,
                kmean_ref, carry_ref, kmax_ref, *, tm):
    i = pl.program_id(0)
    blocks_per_tile = tm // MOBA_BLOCK
    n_blk = kmean_ref.shape[1]

    @pl.when(i == 0)
    def _():
        kmean_ref[...] = jnp.zeros_like(kmean_ref)
        carry_ref[...] = jnp.zeros_like(carry_ref)
        kmax_ref[...] = jnp.zeros_like(kmax_ref)

    h = _rms(x_ref[...], g_ref[...]).astype(BF16)
    cos = cos_ref[...]
    sin = sin_ref[...]

    def rope(t):
        return t * cos + pltpu.roll(t, HEAD_DIM // 2, 1) * sin

    c0 = 3 * ATTN_WIDTH
    slot = lax.broadcasted_iota(jnp.int32, (tm, HEAD_DIM), 1)
    row_blk = (i * tm + lax.broadcasted_iota(jnp.int32, (tm, HEAD_DIM), 0)) // MOBA_BLOCK
    one_hot = (slot == row_blk).astype(BF16)
    q_blk = (i * tm + lax.broadcasted_iota(jnp.int32, (1, tm), 1)) // MOBA_BLOCK
    blk_id = lax.broadcasted_iota(jnp.int32, (n_blk, tm), 0)
    past = blk_id < q_blk

    def project(col, width=2 * HEAD_DIM):
        return jnp.dot(h, w_ref[:, col:col + width], preferred_element_type=F32)

    def key_head(pk_h, hh):
        kr = rope(_rms(pk_h, kg_ref[...]))
        kb = kr.astype(BF16)
        k_ref[hh, :, :HEAD_DIM] = kb
        k_ref[hh, :, HEAD_DIM:] = one_hot
        kn2 = jnp.max(jnp.sum(jnp.square(kb.astype(F32)), axis=1, keepdims=True), axis=0, keepdims=True)
        kmax_ref[hh:hh + 1, :] = jnp.maximum(kmax_ref[hh:hh + 1, :], kn2)
        for b in range(blocks_per_tile):
            blk = kr[b * MOBA_BLOCK:(b + 1) * MOBA_BLOCK]
            kmean_ref[hh, pl.ds(i * blocks_per_tile + b, 1), :] = jnp.mean(blk, axis=0, keepdims=True)

    def memory_head(pm_h, hh):
        sl = slice(hh * HEAD_DIM, (hh + 1) * HEAD_DIM)
        mq = (_rms(pm_h, mqg_ref[...]) * HEAD_DIM ** -0.5).astype(BF16)
        sc = lax.dot_general(mq, mk_ref[:, sl], (((1,), (1,)), ((), ())), preferred_element_type=F32)
        p = jnp.exp(sc - jnp.max(sc, axis=-1, keepdims=True))
        o = jnp.dot(p.astype(BF16), mv_ref[:, sl], preferred_element_type=F32)
        o = o / jnp.sum(p, axis=-1, keepdims=True)
        ycm_ref[:, CONV_CH + hh * HEAD_DIM:CONV_CH + (hh + 1) * HEAD_DIM] = o.astype(BF16)

    def query_head(pq_h, hh):
        qT = rope(_rms(pq_h, qg_ref[...])).T
        km_hi, km_lo = _split_bf16(kmean_ref[hh])
        q_hi, q_lo = _split_bf16(qT)
        gate = jnp.dot(jnp.concatenate([km_hi, km_hi, km_lo], axis=1),
                       jnp.concatenate([q_hi, q_lo, q_hi], axis=0),
                       preferred_element_type=F32)
        gate = jnp.where(past, gate, -jnp.inf)
        bias = jnp.where(blk_id == q_blk, 0.0, MASK_BIAS)
        for _ in range(MOBA_TOPK):
            best = jnp.max(gate, axis=0, keepdims=True)
            first = jnp.min(jnp.where(gate == best, blk_id, n_blk), axis=0, keepdims=True)
            first = jnp.where(best > -jnp.inf, first, -1)
            pick = blk_id == first
            bias = jnp.where(pick, 0.0, bias)
            gate = jnp.where(pick, -jnp.inf, gate)
        qs = (qT * (HEAD_DIM ** -0.5 * LOG2E)).astype(BF16)
        qT_ref[hh, :HEAD_DIM, :] = qs
        qn2 = jnp.sum(jnp.square(qs.astype(F32)), axis=0, keepdims=True)
        bound = jnp.sqrt(qn2 * kmax_ref[hh:hh + 1, 0:1]) * BOUND_SLACK
        bound_ref[hh] = jnp.broadcast_to(bound, (SUBLANES, tm))
        qT_ref[hh, HEAD_DIM:HEAD_DIM + n_blk, :] = bias.astype(BF16)
        qT_ref[hh, HEAD_DIM + n_blk:, :] = jnp.zeros((AUG_DIM - HEAD_DIM - n_blk, tm), BF16)

    pk = project(ATTN_WIDTH, ATTN_WIDTH)
    pq = project(0, ATTN_WIDTH)
    for hh in range(N_ATTN_HEADS):
        key_head(pk[:, hh * HEAD_DIM:(hh + 1) * HEAD_DIM], hh)
    rest = []
    for hh in range(N_ATTN_HEADS):
        if hh % 2 == 0:
            pv2 = project(2 * ATTN_WIDTH + hh * HEAD_DIM)
            for d in range(2):
                vT_ref[hh + d, 0] = pv2[:, d * HEAD_DIM:(d + 1) * HEAD_DIM].T.astype(BF16)
        rest.append(project(c0 + hh * 2 * HEAD_DIM))
        query_head(pq[:, hh * HEAD_DIM:(hh + 1) * HEAD_DIM], hh)

    rest = jnp.concatenate(rest, axis=1)
    pc = rest[:, :3 * CONV_CH]
    pm = rest[:, 3 * CONV_CH:]

    u = pc[:, CONV_CH:2 * CONV_CH] * pc[:, 2 * CONV_CH:]
    prev1 = carry_ref[SUBLANES - 1:SUBLANES, :]
    prev2 = carry_ref[SUBLANES - 2:SUBLANES - 1, :]
    row = lax.broadcasted_iota(jnp.int32, u.shape, 0)
    u1 = jnp.where(row == 0, prev1, pltpu.roll(u, 1, 0))
    u2 = jnp.where(row == 0, prev2, jnp.where(row == 1, prev1, pltpu.roll(u, 2, 0)))
    carry_ref[...] = u[tm - SUBLANES:, :]
    conv = u2 * cw_ref[0:1, :] + u1 * cw_ref[1:2, :] + u * cw_ref[2:3, :]
    ycm_ref[:, :CONV_CH] = (pc[:, :CONV_CH] * conv).astype(BF16)

    for hh in range(N_MEM_HEADS):
        memory_head(pm[:, hh * HEAD_DIM:(hh + 1) * HEAD_DIM], hh)


def _mix(x, gain, w_in, q_gain, k_gain, conv_w, cos, sin, mq_gain, mk, mv, layer, *, tm):
    s = x.shape[0]
    n_blk = s // MOBA_BLOCK
    m = mk.shape[0]
    const = lambda i: (0, 0)
    return pl.pallas_call(
        functools.partial(_mix_kernel, tm=tm),
        grid=(s // tm,),
        in_specs=[pl.BlockSpec((tm, D_MODEL), lambda i: (i, 0)),
                  _resident((1, D_MODEL), const),
                  _resident((None, D_MODEL, IN_WIDTH), lambda i: (layer, 0, 0)),
                  _resident((1, HEAD_DIM), const),
                  _resident((1, HEAD_DIM), const),
                  _resident((CONV_K, CONV_CH), const),
                  pl.BlockSpec((tm, HEAD_DIM), lambda i: (i, 0)),
                  pl.BlockSpec((tm, HEAD_DIM), lambda i: (i, 0)),
                  _resident((1, HEAD_DIM), const),
                  _resident((m, MEM_WIDTH), const),
                  _resident((m, MEM_WIDTH), const)],
        out_specs=[pl.BlockSpec((N_ATTN_HEADS, AUG_DIM, tm), lambda i: (0, 0, i)),
                   pl.BlockSpec((N_ATTN_HEADS, tm, AUG_DIM), lambda i: (0, i, 0)),
                   pl.BlockSpec((N_ATTN_HEADS, 1, HEAD_DIM, tm), lambda i: (0, i, 0, 0)),
                   pl.BlockSpec((tm, CONV_CH + MEM_WIDTH), lambda i: (i, 0)),
                   pl.BlockSpec((N_ATTN_HEADS, SUBLANES, tm), lambda i: (0, 0, i))],
        out_shape=[jax.ShapeDtypeStruct((N_ATTN_HEADS, AUG_DIM, s), BF16),
                   jax.ShapeDtypeStruct((N_ATTN_HEADS, s, AUG_DIM), BF16),
                   jax.ShapeDtypeStruct((N_ATTN_HEADS, s // tm, HEAD_DIM, tm), BF16),
                   jax.ShapeDtypeStruct((s, CONV_CH + MEM_WIDTH), BF16),
                   jax.ShapeDtypeStruct((N_ATTN_HEADS, SUBLANES, s), F32)],
        scratch_shapes=[pltpu.VMEM((N_ATTN_HEADS, n_blk, HEAD_DIM), F32),
                        pltpu.VMEM((SUBLANES, CONV_CH), F32),
                        pltpu.VMEM((N_ATTN_HEADS, HEAD_DIM), F32)],
        compiler_params=_cparams("arbitrary"),
        name="mix_proj",
    )(x, gain, w_in, q_gain, k_gain, conv_w, cos, sin, mq_gain, mk, mv)


Q_TILE = 2048
KEY_TILE = 512
M_INIT = 2 * MASK_BIAS
SAFE_LOGIT_BOUND = 40.0
BOUND_SLACK = 1.0 + 2.0 ** -7


def _causal_fix(s, first_lane_block):
    rows = []
    for b in range(s.shape[0] // MOBA_BLOCK):
        sb = s[b * MOBA_BLOCK:(b + 1) * MOBA_BLOCK]
        lo = (first_lane_block + b) * MOBA_BLOCK
        hi = lo + MOBA_BLOCK
        own = sb[:, lo:hi]
        key_pos = lax.broadcasted_iota(jnp.int32, own.shape, 0)
        qry_pos = lax.broadcasted_iota(jnp.int32, own.shape, 1)
        own = jnp.where(key_pos <= qry_pos, own, MASK_BIAS)
        pieces = ([sb[:, :lo]] if lo else []) + [own] + ([sb[:, hi:]] if hi < s.shape[1] else [])
        rows.append(jnp.concatenate(pieces, axis=1))
    return jnp.concatenate(rows, axis=0)


def _attn_kernel(qT_ref, k_ref, vT_ref, bound_ref, o_ref, acc_ref, l_ref, *, q_tile, key_tile):
    t = pl.program_id(1)
    qT = qT_ref[0]
    acc_ref[...] = jnp.zeros_like(acc_ref)
    l_ref[...] = jnp.zeros_like(l_ref)
    steps_per_tile = q_tile // key_tile
    bound = bound_ref[0, 0:1, :]
    bounded = jnp.max(bound) < SAFE_LOGIT_BOUND

    def scores(g, lane0, diagonal):
        rows = pl.ds(pl.multiple_of(g * key_tile, key_tile), key_tile)
        s = jnp.dot(k_ref[0, rows, :], qT[:, lane0:], preferred_element_type=F32)
        return _causal_fix(s, 0) if diagonal else s

    def update(g, s, m, lane0):
        m_old = m[:, lane0:]
        m_new = jnp.maximum(m_old, jnp.max(s, axis=0, keepdims=True))
        p = jnp.exp2(s - m_new)
        alpha = jnp.exp2(m_old - m_new)
        l_ref[:, lane0:] = l_ref[:, lane0:] * alpha + jnp.sum(p, axis=0, keepdims=True)
        acc_ref[:, lane0:] = (acc_ref[:, lane0:] * alpha
                              + jnp.dot(vT_ref[0, g], p.astype(BF16), preferred_element_type=F32))
        return jnp.concatenate([m[:, :lane0], m_new], axis=1) if lane0 else m_new

    def key_tiles(first, m, diagonal):
        tiles = [(first + d, d * key_tile if diagonal else 0) for d in range(steps_per_tile)]
        ss = [scores(g, lane0, diagonal) for g, lane0 in tiles]
        for (g, lane0), s in zip(tiles, ss):
            m = update(g, s, m, lane0)
        return m

    def bounded_tile(g, lane0, diagonal):
        p = jnp.exp2(scores(g, lane0, diagonal) - bound[:, lane0:])
        l_ref[:, lane0:] += jnp.sum(p, axis=0, keepdims=True)
        acc_ref[:, lane0:] += jnp.dot(vT_ref[0, g], p.astype(BF16), preferred_element_type=F32)

    def bounded_past(u, carry):
        for d in range(steps_per_tile):
            bounded_tile(u * steps_per_tile + d, 0, False)
        return carry

    @pl.when(bounded)
    def _():
        lax.fori_loop(0, t, bounded_past, 0)
        for d in range(steps_per_tile):
            bounded_tile(t * steps_per_tile + d, d * key_tile, True)

    @pl.when(jnp.logical_not(bounded))
    def _():
        m = jnp.full((1, q_tile), M_INIT, F32)
        m = lax.fori_loop(0, t, lambda u, m: key_tiles(u * steps_per_tile, m, False), m)
        key_tiles(t * steps_per_tile, m, True)

    out = acc_ref[...] / l_ref[...]
    o_ref[...] = out.T.astype(o_ref.dtype)


def _attention(qT, k, vT, bound):
    s = k.shape[1]
    key_tile = vT.shape[3]
    q_tile = min(Q_TILE, s)
    return pl.pallas_call(
        functools.partial(_attn_kernel, q_tile=q_tile, key_tile=key_tile),
        grid=(N_ATTN_HEADS, s // q_tile),
        in_specs=[pl.BlockSpec((1, AUG_DIM, q_tile), lambda h, t: (h, 0, t)),
                  pl.BlockSpec((1, s, AUG_DIM), lambda h, t: (h, 0, 0)),
                  pl.BlockSpec((1, s // key_tile, HEAD_DIM, key_tile), lambda h, t: (h, 0, 0, 0)),
                  pl.BlockSpec((1, SUBLANES, q_tile), lambda h, t: (h, 0, t))],
        out_specs=pl.BlockSpec((q_tile, HEAD_DIM), lambda h, t: (t, h)),
        out_shape=jax.ShapeDtypeStruct((s, ATTN_WIDTH), BF16),
        scratch_shapes=[pltpu.VMEM((HEAD_DIM, q_tile), F32), pltpu.VMEM((1, q_tile), F32)],
        compiler_params=_cparams("parallel", "arbitrary"),
        name="moba_attn",
    )(qT, k, vT, bound)


def _outproj_kernel(x_ref, ya_ref, ycm_ref, w_ref, o_ref):
    o_ref[...] = (x_ref[...]
                  + jnp.dot(ya_ref[...], w_ref[:ATTN_WIDTH], preferred_element_type=F32)
                  + jnp.dot(ycm_ref[...], w_ref[ATTN_WIDTH:], preferred_element_type=F32))


def _outproj(x, y_attn, y_cm, w_out, layer, *, tm):
    s = x.shape[0]
    return pl.pallas_call(
        _outproj_kernel,
        grid=(s // tm,),
        in_specs=[pl.BlockSpec((tm, D_MODEL), lambda i: (i, 0)),
                  pl.BlockSpec((tm, ATTN_WIDTH), lambda i: (i, 0)),
                  pl.BlockSpec((tm, CONV_CH + MEM_WIDTH), lambda i: (i, 0)),
                  _resident((None, D_MODEL, D_MODEL), lambda i: (layer, 0, 0))],
        out_specs=pl.BlockSpec((tm, D_MODEL), lambda i: (i, 0)),
        out_shape=jax.ShapeDtypeStruct((s, D_MODEL), F32),
        compiler_params=_cparams("parallel"),
        name="out_proj",
    )(x, y_attn, y_cm, w_out)


def kernel(x, mem, positions, ffn1_norm, ffn1_w_gate_up, ffn1_w_down, mix_norm, w_in, q_norm, k_norm,
           conv_w, mem_norm, w_mem_kv, mq_norm, mk_norm, w_out, ffn2_norm, ffn2_w_gate_up, ffn2_w_down):
    b, s, _ = x.shape
    assert b == 1 and s % KEY_TILE == 0 and s // MOBA_BLOCK <= AUG_DIM - HEAD_DIM
    depth = w_in.shape[0]
    tm = min(s, KEY_TILE)
    ffn = functools.partial(_ffn, tm=min(s, ROW_TILE), tf=FFN_FF_TILE)

    ffn_weights = [(ffn1_w_gate_up, ffn1_w_down), (ffn2_w_gate_up, ffn2_w_down)]
    w_gu, w_d = (w[0].astype(BF16) for w in ffn_weights[0])
    ffn_gains = [ffn1_norm, ffn2_norm]
    w_in_b, w_out_b, w_mkv = (w.astype(BF16) for w in (w_in, w_out, w_mem_kv))
    cos, sin = _rope_tables(positions[0])
    xs = x[0]
    mem2 = mem[0]

    def ffn_call(xs, which, l, w_gu, w_d):
        nxt = 2 * l + which + 1
        next_weights = None if nxt == 2 * depth else ffn_weights[nxt % 2] + (nxt // 2,)
        out = ffn(xs, ffn_gains[which][l][None], w_gu, w_d, next_weights)
        return tuple(out) if next_weights else (out[0], None, None)

    for l in range(depth):
        xs, w_gu, w_d = ffn_call(xs, 0, l, w_gu, w_d)
        mk, mv = _memkv(mem2, mem_norm[l][None], w_mkv, mk_norm[l][None], l)
        qT, k, vT, y_cm, bound = _mix(xs, mix_norm[l][None], w_in_b, q_norm[l][None],
                                      k_norm[l][None], conv_w[l], cos, sin, mq_norm[l][None],
                                      mk, mv, l, tm=tm)
        y_attn = _attention(qT, k, vT, bound)
        xs = _outproj(xs, y_attn, y_cm, w_out_b, l, tm=min(s, ROW_TILE))
        xs, w_gu, w_d = ffn_call(xs, 1, l, w_gu, w_d)
    return xs[None]
```

```python
import functools
import math

import jax
import jax.numpy as jnp
from jax import lax
from jax.experimental import pallas as pl
from jax.experimental.pallas import tpu as pltpu

F32 = jnp.float32
BF16 = jnp.bfloat16

D_MODEL = 2048
HEAD_DIM = 128
ATTN_WIDTH = 1024
N_ATTN_HEADS = 8
CONV_CH = 512
CONV_K = 3
MEM_WIDTH = 512
N_MEM_HEADS = 4
IN_WIDTH = 3 * ATTN_WIDTH + 3 * CONV_CH + MEM_WIDTH
D_FF = 5632
FFN_RES = 0.5
MOBA_BLOCK = 256
MOBA_TOPK = 3
ROPE_THETA = 10000.0
RMS_EPS = 1e-6

V7X_VMEM_BYTES = 64 * 1024 * 1024
VMEM_LIMIT = V7X_VMEM_BYTES - 6 * 1024 * 1024
SUBLANES = 8

MASK_BIAS = -1e30
AUG_DIM = 2 * HEAD_DIM
LOG2E = math.log2(math.e)


def _rms(x, gain):
    return x * lax.rsqrt(jnp.mean(x * x, axis=-1, keepdims=True) + RMS_EPS) * gain


def _split_bf16(a):
    hi = a.astype(BF16)
    return hi, (a - hi.astype(F32)).astype(BF16)


def _cparams(*sem):
    return pltpu.CompilerParams(dimension_semantics=sem, vmem_limit_bytes=VMEM_LIMIT)


def _resident(shape, index_map):
    return pl.BlockSpec(shape, index_map, pipeline_mode=pl.Buffered(1))


ROPE_ROW_TILE = 2048


def _rope_kernel(pos_ref, inv_ref, cos_ref, sin_ref):
    ang = pos_ref[...].astype(F32) * inv_ref[...]
    lane = lax.broadcasted_iota(jnp.int32, ang.shape, 1)
    s = jnp.sin(ang)
    cos_ref[...] = jnp.cos(ang)
    sin_ref[...] = jnp.where(lane < HEAD_DIM // 2, -s, s)


def _rope_tables(positions):
    s = positions.shape[0]
    t = min(s, ROPE_ROW_TILE)
    inv = ROPE_THETA ** (-jnp.arange(0, HEAD_DIM, 2, dtype=F32) / HEAD_DIM)
    inv = jnp.concatenate([inv, inv])[None, :]
    return pl.pallas_call(
        _rope_kernel,
        grid=(s // t,),
        in_specs=[pl.BlockSpec((t, 1), lambda i: (i, 0)),
                  pl.BlockSpec((1, HEAD_DIM), lambda i: (0, 0))],
        out_specs=[pl.BlockSpec((t, HEAD_DIM), lambda i: (i, 0))] * 2,
        out_shape=[jax.ShapeDtypeStruct((s, HEAD_DIM), F32)] * 2,
        compiler_params=_cparams("parallel"),
        name="rope_tables",
    )(positions[:, None], inv)


ROW_TILE = 1024
FFN_FF_TILE = 512


def _ffn_kernel(*refs, convert_next, tf):
    if convert_next:
        x_ref, g_ref, wg_ref, wu_ref, wd_ref, ngu_ref, nd_ref, o_ref, ngu_out, nd_out, h_ref = refs
    else:
        x_ref, g_ref, wg_ref, wu_ref, wd_ref, o_ref, h_ref = refs
    j = pl.program_id(1)

    def half_step(h):
        gate = jnp.dot(h, wg_ref[...], preferred_element_type=F32)
        up = jnp.dot(h, wu_ref[...], preferred_element_type=F32)
        act = (gate * jax.nn.sigmoid(gate) * up * FFN_RES).astype(BF16)
        if convert_next:
            for c in range(ngu_out.shape[0]):
                ngu_out[c] = ngu_ref[:, c * tf:(c + 1) * tf].astype(BF16)
            nd_out[...] = nd_ref[...].astype(BF16)
        return jnp.dot(act, wd_ref[...], preferred_element_type=F32)

    @pl.when(j == 0)
    def _():
        x = x_ref[...]
        h = _rms(x, g_ref[...]).astype(BF16)
        h_ref[...] = h
        o_ref[...] = x + half_step(h)

    @pl.when(j > 0)
    def _():
        o_ref[...] += half_step(h_ref[...])


def _ffn(x, gain, w_gate_up, w_down, next_weights=None, *, tm, tf):
    s = x.shape[0]
    n_i, n_f = s // tm, D_FF // tf
    in_specs = [pl.BlockSpec((tm, D_MODEL), lambda i, j: (i, 0)),
                pl.BlockSpec((1, D_MODEL), lambda i, j: (0, 0)),
                pl.BlockSpec((None, D_MODEL, tf), lambda i, j: (j, 0, 0)),
                pl.BlockSpec((None, D_MODEL, tf), lambda i, j: (j + n_f, 0, 0)),
                pl.BlockSpec((tf, D_MODEL), lambda i, j: (j, 0))]
    out_specs = [pl.BlockSpec((tm, D_MODEL), lambda i, j: (i, 0))]
    out_shape = [jax.ShapeDtypeStruct((s, D_MODEL), F32)]
    operands = [x, gain, w_gate_up, w_gate_up, w_down]
    if next_weights is not None:
        gu32, d32, layer = next_weights
        gu_blk = (D_MODEL // n_i, 2 * D_FF // n_f)
        d_blk = (D_FF // (n_i * n_f), D_MODEL)
        in_specs += [pl.BlockSpec((None,) + gu_blk, lambda i, j: (layer, i, j)),
                     pl.BlockSpec((None,) + d_blk, lambda i, j: (layer, i * n_f + j, 0))]
        tiles_per_slice = gu_blk[1] // tf
        out_specs += [pl.BlockSpec((tiles_per_slice, gu_blk[0], tf), lambda i, j: (j, i, 0)),
                      pl.BlockSpec(d_blk, lambda i, j: (i * n_f + j, 0))]
        out_shape += [jax.ShapeDtypeStruct((2 * n_f, D_MODEL, tf), BF16),
                      jax.ShapeDtypeStruct(d32.shape[1:], BF16)]
        operands += [gu32, d32]
    return pl.pallas_call(
        functools.partial(_ffn_kernel, convert_next=next_weights is not None, tf=tf),
        grid=(n_i, n_f),
        in_specs=in_specs,
        out_specs=out_specs,
        out_shape=out_shape,
        scratch_shapes=[pltpu.VMEM((tm, D_MODEL), BF16)],
        compiler_params=_cparams("parallel", "arbitrary"),
        name="ffn",
    )(*operands)


def _memkv_kernel(mem_ref, g_ref, w_ref, kg_ref, mk_ref, mv_ref):
    h = _rms(mem_ref[...], g_ref[...]).astype(BF16)
    kv = jnp.dot(h, w_ref[...], preferred_element_type=F32)
    for hh in range(N_MEM_HEADS):
        sl = slice(hh * HEAD_DIM, (hh + 1) * HEAD_DIM)
        mk_ref[:, sl] = _rms(kv[:, sl], kg_ref[...]).astype(BF16)
    mv_ref[...] = kv[:, MEM_WIDTH:].astype(BF16)


def _memkv(mem, gain, w_mem_kv, k_gain, layer):
    m = mem.shape[0]
    const = lambda i: (0, 0)
    return pl.pallas_call(
        _memkv_kernel,
        grid=(1,),
        in_specs=[pl.BlockSpec((m, D_MODEL), const),
                  pl.BlockSpec((1, D_MODEL), const),
                  pl.BlockSpec((None, D_MODEL, 2 * MEM_WIDTH), lambda i: (layer, 0, 0)),
                  pl.BlockSpec((1, HEAD_DIM), const)],
        out_specs=[pl.BlockSpec((m, MEM_WIDTH), const)] * 2,
        out_shape=[jax.ShapeDtypeStruct((m, MEM_WIDTH), BF16)] * 2,
        compiler_params=_cparams("arbitrary"),
        name="memkv",
    )(mem, gain, w_mem_kv, k_gain)


def _mix_kernel(x_ref, g_ref, w_ref, qg_ref, kg_ref, cw_ref, cos_ref, sin_ref, mqg_ref,
                mk_ref, mv_ref, qT_ref, k_ref, vT_ref, ycm_ref, bound_ref,
                kmean_ref, carry_ref, kmax_ref, *, tm):
    i = pl.program_id(0)
    blocks_per_tile = tm // MOBA_BLOCK
    n_blk = kmean_ref.shape[1]

    @pl.when(i == 0)
    def _():
        kmean_ref[...] = jnp.zeros_like(kmean_ref)
        carry_ref[...] = jnp.zeros_like(carry_ref)
        kmax_ref[...] = jnp.zeros_like(kmax_ref)

    h = _rms(x_ref[...], g_ref[...]).astype(BF16)
    cos = cos_ref[...]
    sin = sin_ref[...]

    def rope(t):
        return t * cos + pltpu.roll(t, HEAD_DIM // 2, 1) * sin

    c0 = 3 * ATTN_WIDTH
    slot = lax.broadcasted_iota(jnp.int32, (tm, HEAD_DIM), 1)
    row_blk = (i * tm + lax.broadcasted_iota(jnp.int32, (tm, HEAD_DIM), 0)) // MOBA_BLOCK
    one_hot = (slot == row_blk).astype(BF16)
    q_blk = (i * tm + lax.broadcasted_iota(jnp.int32, (1, tm), 1)) // MOBA_BLOCK
    blk_id = lax.broadcasted_iota(jnp.int32, (n_blk, tm), 0)
    past = blk_id < q_blk

    def project(col, width=2 * HEAD_DIM):
        return jnp.dot(h, w_ref[:, col:col + width], preferred_element_type=F32)

    def key_head(pk_h, hh):
        kr = rope(_rms(pk_h, kg_ref[...]))
        kb = kr.astype(BF16)
        k_ref[hh, :, :HEAD_DIM] = kb
        k_ref[hh, :, HEAD_DIM:] = one_hot
        kn2 = jnp.max(jnp.sum(jnp.square(kb.astype(F32)), axis=1, keepdims=True), axis=0, keepdims=True)
        kmax_ref[hh:hh + 1, :] = jnp.maximum(kmax_ref[hh:hh + 1, :], kn2)
        for b in range(blocks_per_tile):
            blk = kr[b * MOBA_BLOCK:(b + 1) * MOBA_BLOCK]
            kmean_ref[hh, pl.ds(i * blocks_per_tile + b, 1), :] = jnp.mean(blk, axis=0, keepdims=True)

    def memory_head(pm_h, hh):
        sl = slice(hh * HEAD_DIM, (hh + 1) * HEAD_DIM)
        mq = (_rms(pm_h, mqg_ref[...]) * HEAD_DIM ** -0.5).astype(BF16)
        sc = lax.dot_general(mq, mk_ref[:, sl], (((1,), (1,)), ((), ())), preferred_element_type=F32)
        p = jnp.exp(sc - jnp.max(sc, axis=-1, keepdims=True))
        o = jnp.dot(p.astype(BF16), mv_ref[:, sl], preferred_element_type=F32)
        o = o / jnp.sum(p, axis=-1, keepdims=True)
        ycm_ref[:, CONV_CH + hh * HEAD_DIM:CONV_CH + (hh + 1) * HEAD_DIM] = o.astype(BF16)

    def query_head(pq_h, hh):
        qT = rope(_rms(pq_h, qg_ref[...])).T
        km_hi, km_lo = _split_bf16(kmean_ref[hh])
        q_hi, q_lo = _split_bf16(qT)
        gate = jnp.dot(jnp.concatenate([km_hi, km_hi, km_lo], axis=1),
                       jnp.concatenate([q_hi, q_lo, q_hi], axis=0),
                       preferred_element_type=F32)
        gate = jnp.where(past, gate, -jnp.inf)
        bias = jnp.where(blk_id == q_blk, 0.0, MASK_BIAS)
        for _ in range(MOBA_TOPK):
            best = jnp.max(gate, axis=0, keepdims=True)
            first = jnp.min(jnp.where(gate == best, blk_id, n_blk), axis=0, keepdims=True)
            first = jnp.where(best > -jnp.inf, first, -1)
            pick = blk_id == first
            bias = jnp.where(pick, 0.0, bias)
            gate = jnp.where(pick, -jnp.inf, gate)
        qs = (qT * (HEAD_DIM ** -0.5 * LOG2E)).astype(BF16)
        qT_ref[hh, :HEAD_DIM, :] = qs
        qn2 = jnp.sum(jnp.square(qs.astype(F32)), axis=0, keepdims=True)
        bound = jnp.sqrt(qn2 * kmax_ref[hh:hh + 1, 0:1]) * BOUND_SLACK
        bound_ref[hh] = jnp.broadcast_to(bound, (SUBLANES, tm))
        qT_ref[hh, HEAD_DIM:HEAD_DIM + n_blk, :] = bias.astype(BF16)
        qT_ref[hh, HEAD_DIM + n_blk:, :] = jnp.zeros((AUG_DIM - HEAD_DIM - n_blk, tm), BF16)

    pk = project(ATTN_WIDTH, ATTN_WIDTH)
    pq = project(0, ATTN_WIDTH)
    for hh in range(N_ATTN_HEADS):
        key_head(pk[:, hh * HEAD_DIM:(hh + 1) * HEAD_DIM], hh)
    rest = []
    for hh in range(N_ATTN_HEADS):
        if hh % 2 == 0:
            pv2 = project(2 * ATTN_WIDTH + hh * HEAD_DIM)
            for d in range(2):
                vT_ref[hh + d, 0] = pv2[:, d * HEAD_DIM:(d + 1) * HEAD_DIM].T.astype(BF16)
        rest.append(project(c0 + hh * 2 * HEAD_DIM))
        query_head(pq[:, hh * HEAD_DIM:(hh + 1) * HEAD_DIM], hh)

    rest = jnp.concatenate(rest, axis=1)
    pc = rest[:, :3 * CONV_CH]
    pm = rest[:, 3 * CONV_CH:]

    u = pc[:, CONV_CH:2 * CONV_CH] * pc[:, 2 * CONV_CH:]
    prev1 = carry_ref[SUBLANES - 1:SUBLANES, :]
    prev2 = carry_ref[SUBLANES - 2:SUBLANES - 1, :]
    row = lax.broadcasted_iota(jnp.int32, u.shape, 0)
    u1 = jnp.where(row == 0, prev1, pltpu.roll(u, 1, 0))
    u2 = jnp.where(row == 0, prev2, jnp.where(row == 1, prev1, pltpu.roll(u, 2, 0)))
    carry_ref[...] = u[tm - SUBLANES:, :]
    conv = u2 * cw_ref[0:1, :] + u1 * cw_ref[1:2, :] + u * cw_ref[2:3, :]
    ycm_ref[:, :CONV_CH] = (pc[:, :CONV_CH] * conv).astype(BF16)

    for hh in range(N_MEM_HEADS):
        memory_head(pm[:, hh * HEAD_DIM:(hh + 1) * HEAD_DIM], hh)


def _mix(x, gain, w_in, q_gain, k_gain, conv_w, cos, sin, mq_gain, mk, mv, layer, *, tm):
    s = x.shape[0]
    n_blk = s // MOBA_BLOCK
    m = mk.shape[0]
    const = lambda i: (0, 0)
    return pl.pallas_call(
        functools.partial(_mix_kernel, tm=tm),
        grid=(s // tm,),
        in_specs=[pl.BlockSpec((tm, D_MODEL), lambda i: (i, 0)),
                  _resident((1, D_MODEL), const),
                  _resident((None, D_MODEL, IN_WIDTH), lambda i: (layer, 0, 0)),
                  _resident((1, HEAD_DIM), const),
                  _resident((1, HEAD_DIM), const),
                  _resident((CONV_K, CONV_CH), const),
                  pl.BlockSpec((tm, HEAD_DIM), lambda i: (i, 0)),
                  pl.BlockSpec((tm, HEAD_DIM), lambda i: (i, 0)),
                  _resident((1, HEAD_DIM), const),
                  _resident((m, MEM_WIDTH), const),
                  _resident((m, MEM_WIDTH), const)],
        out_specs=[pl.BlockSpec((N_ATTN_HEADS, AUG_DIM, tm), lambda i: (0, 0, i)),
                   pl.BlockSpec((N_ATTN_HEADS, tm, AUG_DIM), lambda i: (0, i, 0)),
                   pl.BlockSpec((N_ATTN_HEADS, 1, HEAD_DIM, tm), lambda i: (0, i, 0, 0)),
                   pl.BlockSpec((tm, CONV_CH + MEM_WIDTH), lambda i: (i, 0)),
                   pl.BlockSpec((N_ATTN_HEADS, SUBLANES, tm), lambda i: (0, 0, i))],
        out_shape=[jax.ShapeDtypeStruct((N_ATTN_HEADS, AUG_DIM, s), BF16),
                   jax.ShapeDtypeStruct((N_ATTN_HEADS, s, AUG_DIM), BF16),
                   jax.ShapeDtypeStruct((N_ATTN_HEADS, s // tm, HEAD_DIM, tm), BF16),
                   jax.ShapeDtypeStruct((s, CONV_CH + MEM_WIDTH), BF16),
                   jax.ShapeDtypeStruct((N_ATTN_HEADS, SUBLANES, s), F32)],
        scratch_shapes=[pltpu.VMEM((N_ATTN_HEADS, n_blk, HEAD_DIM), F32),
                        pltpu.VMEM((SUBLANES, CONV_CH), F32),
                        pltpu.VMEM((N_ATTN_HEADS, HEAD_DIM), F32)],
        compiler_params=_cparams("arbitrary"),
        name="mix_proj",
    )(x, gain, w_in, q_gain, k_gain, conv_w, cos, sin, mq_gain, mk, mv)


Q_TILE = 2048
KEY_TILE = 512
M_INIT = 2 * MASK_BIAS
SAFE_LOGIT_BOUND = 40.0
BOUND_SLACK = 1.0 + 2.0 ** -7


def _causal_fix(s, first_lane_block):
    rows = []
    for b in range(s.shape[0] // MOBA_BLOCK):
        sb = s[b * MOBA_BLOCK:(b + 1) * MOBA_BLOCK]
        lo = (first_lane_block + b) * MOBA_BLOCK
        hi = lo + MOBA_BLOCK
        own = sb[:, lo:hi]
        key_pos = lax.broadcasted_iota(jnp.int32, own.shape, 0)
        qry_pos = lax.broadcasted_iota(jnp.int32, own.shape, 1)
        own = jnp.where(key_pos <= qry_pos, own, MASK_BIAS)
        pieces = ([sb[:, :lo]] if lo else []) + [own] + ([sb[:, hi:]] if hi < s.shape[1] else [])
        rows.append(jnp.concatenate(pieces, axis=1))
    return jnp.concatenate(rows, axis=0)


def _attn_kernel(qT_ref, k_ref, vT_ref, bound_ref, o_ref, acc_ref, l_ref, *, q_tile, key_tile):
    t = pl.program_id(1)
    qT = qT_ref[0]
    acc_ref[...] = jnp.zeros_like(acc_ref)
    l_ref[...] = jnp.zeros_like(l_ref)
    steps_per_tile = q_tile // key_tile
    bound = bound_ref[0, 0:1, :]
    bounded = jnp.max(bound) < SAFE_LOGIT_BOUND

    def scores(g, lane0, diagonal):
        rows = pl.ds(pl.multiple_of(g * key_tile, key_tile), key_tile)
        s = jnp.dot(k_ref[0, rows, :], qT[:, lane0:], preferred_element_type=F32)
        return _causal_fix(s, 0) if diagonal else s

    def update(g, s, m, lane0):
        m_old = m[:, lane0:]
        m_new = jnp.maximum(m_old, jnp.max(s, axis=0, keepdims=True))
        p = jnp.exp2(s - m_new)
        alpha = jnp.exp2(m_old - m_new)
        l_ref[:, lane0:] = l_ref[:, lane0:] * alpha + jnp.sum(p, axis=0, keepdims=True)
        acc_ref[:, lane0:] = (acc_ref[:, lane0:] * alpha
                              + jnp.dot(vT_ref[0, g], p.astype(BF16), preferred_element_type=F32))
        return jnp.concatenate([m[:, :lane0], m_new], axis=1) if lane0 else m_new

    def key_tiles(first, m, diagonal):
        tiles = [(first + d, d * key_tile if diagonal else 0) for d in range(steps_per_tile)]
        ss = [scores(g, lane0, diagonal) for g, lane0 in tiles]
        for (g, lane0), s in zip(tiles, ss):
            m = update(g, s, m, lane0)
        return m

    def bounded_tile(g, lane0, diagonal):
        p = jnp.exp2(scores(g, lane0, diagonal) - bound[:, lane0:])
        l_ref[:, lane0:] += jnp.sum(p, axis=0, keepdims=True)
        acc_ref[:, lane0:] += jnp.dot(vT_ref[0, g], p.astype(BF16), preferred_element_type=F32)

    def bounded_past(u, carry):
        for d in range(steps_per_tile):
            bounded_tile(u * steps_per_tile + d, 0, False)
        return carry

    @pl.when(bounded)
    def _():
        lax.fori_loop(0, t, bounded_past, 0)
        for d in range(steps_per_tile):
            bounded_tile(t * steps_per_tile + d, d * key_tile, True)

    @pl.when(jnp.logical_not(bounded))
    def _():
        m = jnp.full((1, q_tile), M_INIT, F32)
        m = lax.fori_loop(0, t, lambda u, m: key_tiles(u * steps_per_tile, m, False), m)
        key_tiles(t * steps_per_tile, m, True)

    out = acc_ref[...] / l_ref[...]
    o_ref[...] = out.T.astype(o_ref.dtype)


def _attention(qT, k, vT, bound):
    s = k.shape[1]
    key_tile = vT.shape[3]
    q_tile = min(Q_TILE, s)
    return pl.pallas_call(
        functools.partial(_attn_kernel, q_tile=q_tile, key_tile=key_tile),
        grid=(N_ATTN_HEADS, s // q_tile),
        in_specs=[pl.BlockSpec((1, AUG_DIM, q_tile), lambda h, t: (h, 0, t)),
                  pl.BlockSpec((1, s, AUG_DIM), lambda h, t: (h, 0, 0)),
                  pl.BlockSpec((1, s // key_tile, HEAD_DIM, key_tile), lambda h, t: (h, 0, 0, 0)),
                  pl.BlockSpec((1, SUBLANES, q_tile), lambda h, t: (h, 0, t))],
        out_specs=pl.BlockSpec((q_tile, HEAD_DIM), lambda h, t: (t, h)),
        out_shape=jax.ShapeDtypeStruct((s, ATTN_WIDTH), BF16),
        scratch_shapes=[pltpu.VMEM((HEAD_DIM, q_tile), F32), pltpu.VMEM((1, q_tile), F32)],
        compiler_params=_cparams("parallel", "arbitrary"),
        name="moba_attn",
    )(qT, k, vT, bound)


def _outproj_kernel(x_ref, ya_ref, ycm_ref, w_ref, o_ref):
    o_ref[...] = (x_ref[...]
                  + jnp.dot(ya_ref[...], w_ref[:ATTN_WIDTH], preferred_element_type=F32)
                  + jnp.dot(ycm_ref[...], w_ref[ATTN_WIDTH:], preferred_element_type=F32))


def _outproj(x, y_attn, y_cm, w_out, layer, *, tm):
    s = x.shape[0]
    return pl.pallas_call(
        _outproj_kernel,
        grid=(s // tm,),
        in_specs=[pl.BlockSpec((tm, D_MODEL), lambda i: (i, 0)),
                  pl.BlockSpec((tm, ATTN_WIDTH), lambda i: (i, 0)),
                  pl.BlockSpec((tm, CONV_CH + MEM_WIDTH), lambda i: (i, 0)),
                  _resident((None, D_MODEL, D_MODEL), lambda i: (layer, 0, 0))],
        out_specs=pl.BlockSpec((tm, D_MODEL), lambda i: (i, 0)),
        out_shape=jax.ShapeDtypeStruct((s, D_MODEL), F32),
        compiler_params=_cparams("parallel"),
        name="out_proj",
    )(x, y_attn, y_cm, w_out)


def kernel(x, mem, positions, ffn1_norm, ffn1_w_gate_up, ffn1_w_down, mix_norm, w_in, q_norm, k_norm,
           conv_w, mem_norm, w_mem_kv, mq_norm, mk_norm, w_out, ffn2_norm, ffn2_w_gate_up, ffn2_w_down):
    b, s, _ = x.shape
    assert b == 1 and s % KEY_TILE == 0 and s // MOBA_BLOCK <= AUG_DIM - HEAD_DIM
    depth = w_in.shape[0]
    tm = min(s, KEY_TILE)
    ffn = functools.partial(_ffn, tm=min(s, ROW_TILE), tf=FFN_FF_TILE)

    ffn_weights = [(ffn1_w_gate_up, ffn1_w_down), (ffn2_w_gate_up, ffn2_w_down)]
    w_gu = (ffn1_w_gate_up[0].reshape(D_MODEL, 2 * D_FF // FFN_FF_TILE, FFN_FF_TILE)
            .transpose(1, 0, 2).astype(BF16))
    w_d = ffn1_w_down[0].astype(BF16)
    ffn_gains = [ffn1_norm, ffn2_norm]
    w_in_b, w_out_b, w_mkv = (w.astype(BF16) for w in (w_in, w_out, w_mem_kv))
    cos, sin = _rope_tables(positions[0])
    xs = x[0]
    mem2 = mem[0]

    def ffn_call(xs, which, l, w_gu, w_d):
        nxt = 2 * l + which + 1
        next_weights = None if nxt == 2 * depth else ffn_weights[nxt % 2] + (nxt // 2,)
        out = ffn(xs, ffn_gains[which][l][None], w_gu, w_d, next_weights)
        return tuple(out) if next_weights else (out[0], None, None)

    for l in range(depth):
        xs, w_gu, w_d = ffn_call(xs, 0, l, w_gu, w_d)
        mk, mv = _memkv(mem2, mem_norm[l][None], w_mkv, mk_norm[l][None], l)
        qT, k, vT, y_cm, bound = _mix(xs, mix_norm[l][None], w_in_b, q_norm[l][None],
                                      k_norm[l][None], conv_w[l], cos, sin, mq_norm[l][None],
                                      mk, mv, l, tm=tm)
        y_attn = _attention(qT, k, vT, bound)
        xs = _outproj(xs, y_attn, y_cm, w_out_b, l, tm=min(s, ROW_TILE))
        xs, w_gu, w_d = ffn_call(xs, 1, l, w_gu, w_d)
    return xs[None]
```

```python
import functools
import math

import jax
import jax.numpy as jnp
from jax import lax
from jax.experimental import pallas as pl
from jax.experimental.pallas import tpu as pltpu

F32 = jnp.float32
BF16 = jnp.bfloat16

D_MODEL = 2048
HEAD_DIM = 128
ATTN_WIDTH = 1024
N_ATTN_HEADS = 8
CONV_CH = 512
CONV_K = 3
MEM_WIDTH = 512
N_MEM_HEADS = 4
IN_WIDTH = 3 * ATTN_WIDTH + 3 * CONV_CH + MEM_WIDTH
D_FF = 5632
FFN_RES = 0.5
MOBA_BLOCK = 256
MOBA_TOPK = 3
ROPE_THETA = 10000.0
RMS_EPS = 1e-6

V7X_VMEM_BYTES = 64 * 1024 * 1024
VMEM_LIMIT = V7X_VMEM_BYTES - 6 * 1024 * 1024
SUBLANES = 8

MASK_BIAS = -1e30
AUG_DIM = 2 * HEAD_DIM
LOG2E = math.log2(math.e)


def _rms(x, gain):
    return x * lax.rsqrt(jnp.mean(x * x, axis=-1, keepdims=True) + RMS_EPS) * gain


def _split_bf16(a):
    hi = a.astype(BF16)
    return hi, (a - hi.astype(F32)).astype(BF16)


def _cparams(*sem):
    return pltpu.CompilerParams(dimension_semantics=sem, vmem_limit_bytes=VMEM_LIMIT)


def _resident(shape, index_map):
    return pl.BlockSpec(shape, index_map, pipeline_mode=pl.Buffered(1))


ROPE_ROW_TILE = 2048


def _rope_kernel(pos_ref, inv_ref, cos_ref, sin_ref):
    ang = pos_ref[...].astype(F32) * inv_ref[...]
    lane = lax.broadcasted_iota(jnp.int32, ang.shape, 1)
    s = jnp.sin(ang)
    cos_ref[...] = jnp.cos(ang)
    sin_ref[...] = jnp.where(lane < HEAD_DIM // 2, -s, s)


def _rope_tables(positions):
    s = positions.shape[0]
    t = min(s, ROPE_ROW_TILE)
    inv = ROPE_THETA ** (-jnp.arange(0, HEAD_DIM, 2, dtype=F32) / HEAD_DIM)
    inv = jnp.concatenate([inv, inv])[None, :]
    return pl.pallas_call(
        _rope_kernel,
        grid=(s // t,),
        in_specs=[pl.BlockSpec((t, 1), lambda i: (i, 0)),
                  pl.BlockSpec((1, HEAD_DIM), lambda i: (0, 0))],
        out_specs=[pl.BlockSpec((t, HEAD_DIM), lambda i: (i, 0))] * 2,
        out_shape=[jax.ShapeDtypeStruct((s, HEAD_DIM), F32)] * 2,
        compiler_params=_cparams("parallel"),
        name="rope_tables",
    )(positions[:, None], inv)


ROW_TILE = 1024
FFN_FF_TILE = 512


def _ffn_kernel(*refs, convert_next):
    if convert_next:
        x_ref, g_ref, wg_ref, wu_ref, wd_ref, ngu_ref, nd_ref, o_ref, ngu_out, nd_out, h_ref = refs
    else:
        x_ref, g_ref, wg_ref, wu_ref, wd_ref, o_ref, h_ref = refs
    j = pl.program_id(1)

    def half_step(h):
        gate = jnp.dot(h, wg_ref[...], preferred_element_type=F32)
        up = jnp.dot(h, wu_ref[...], preferred_element_type=F32)
        act = (gate * jax.nn.sigmoid(gate) * up * FFN_RES).astype(BF16)
        if convert_next:
            ngu_out[...] = ngu_ref[...].astype(BF16)
            nd_out[...] = nd_ref[...].astype(BF16)
        return jnp.dot(act, wd_ref[...], preferred_element_type=F32)

    @pl.when(j == 0)
    def _():
        x = x_ref[...]
        h = _rms(x, g_ref[...]).astype(BF16)
        h_ref[...] = h
        o_ref[...] = x + half_step(h)

    @pl.when(j > 0)
    def _():
        o_ref[...] += half_step(h_ref[...])


def _ffn(x, gain, w_gate_up, w_down, next_weights=None, *, tm, tf):
    s = x.shape[0]
    n_i, n_f = s // tm, D_FF // tf
    in_specs = [pl.BlockSpec((tm, D_MODEL), lambda i, j: (i, 0)),
                pl.BlockSpec((1, D_MODEL), lambda i, j: (0, 0)),
                pl.BlockSpec((D_MODEL, tf), lambda i, j: (0, j)),
                pl.BlockSpec((D_MODEL, tf), lambda i, j: (0, j + n_f)),
                pl.BlockSpec((tf, D_MODEL), lambda i, j: (j, 0))]
    out_specs = [pl.BlockSpec((tm, D_MODEL), lambda i, j: (i, 0))]
    out_shape = [jax.ShapeDtypeStruct((s, D_MODEL), F32)]
    operands = [x, gain, w_gate_up, w_gate_up, w_down]
    if next_weights is not None:
        gu32, d32, layer = next_weights
        gu_blk = (D_MODEL // n_i, 2 * D_FF // n_f)
        d_blk = (D_FF // (n_i * n_f), D_MODEL)
        in_specs += [pl.BlockSpec((None,) + gu_blk, lambda i, j: (layer, i, j)),
                     pl.BlockSpec((None,) + d_blk, lambda i, j: (layer, i * n_f + j, 0))]
        out_specs += [pl.BlockSpec(gu_blk, lambda i, j: (i, j)),
                      pl.BlockSpec(d_blk, lambda i, j: (i * n_f + j, 0))]
        out_shape += [jax.ShapeDtypeStruct(gu32.shape[1:], BF16), jax.ShapeDtypeStruct(d32.shape[1:], BF16)]
        operands += [gu32, d32]
    return pl.pallas_call(
        functools.partial(_ffn_kernel, convert_next=next_weights is not None),
        grid=(n_i, n_f),
        in_specs=in_specs,
        out_specs=out_specs,
        out_shape=out_shape,
        scratch_shapes=[pltpu.VMEM((tm, D_MODEL), BF16)],
        compiler_params=_cparams("parallel", "arbitrary"),
        name="ffn",
    )(*operands)


def _memkv_kernel(mem_ref, g_ref, w_ref, kg_ref, mk_ref, mv_ref):
    h = _rms(mem_ref[...], g_ref[...]).astype(BF16)
    kv = jnp.dot(h, w_ref[...], preferred_element_type=F32)
    for hh in range(N_MEM_HEADS):
        sl = slice(hh * HEAD_DIM, (hh + 1) * HEAD_DIM)
        mk_ref[:, sl] = _rms(kv[:, sl], kg_ref[...]).astype(BF16)
    mv_ref[...] = kv[:, MEM_WIDTH:].astype(BF16)


def _memkv(mem, gain, w_mem_kv, k_gain, layer):
    m = mem.shape[0]
    const = lambda i: (0, 0)
    return pl.pallas_call(
        _memkv_kernel,
        grid=(1,),
        in_specs=[pl.BlockSpec((m, D_MODEL), const),
                  pl.BlockSpec((1, D_MODEL), const),
                  pl.BlockSpec((None, D_MODEL, 2 * MEM_WIDTH), lambda i: (layer, 0, 0)),
                  pl.BlockSpec((1, HEAD_DIM), const)],
        out_specs=[pl.BlockSpec((m, MEM_WIDTH), const)] * 2,
        out_shape=[jax.ShapeDtypeStruct((m, MEM_WIDTH), BF16)] * 2,
        compiler_params=_cparams("arbitrary"),
        name="memkv",
    )(mem, gain, w_mem_kv, k_gain)


def _mix_kernel(x_ref, g_ref, w_ref, qg_ref, kg_ref, cw_ref, cos_ref, sin_ref, mqg_ref,
                mk_ref, mv_ref, qT_ref, k_ref, vT_ref, ycm_ref, bound_ref,
                kmean_ref, carry_ref, kmax_ref, *, tm):
    i = pl.program_id(0)
    blocks_per_tile = tm // MOBA_BLOCK
    n_blk = kmean_ref.shape[1]

    @pl.when(i == 0)
    def _():
        kmean_ref[...] = jnp.zeros_like(kmean_ref)
        carry_ref[...] = jnp.zeros_like(carry_ref)
        kmax_ref[...] = jnp.zeros_like(kmax_ref)

    h = _rms(x_ref[...], g_ref[...]).astype(BF16)
    cos = cos_ref[...]
    sin = sin_ref[...]

    def rope(t):
        return t * cos + pltpu.roll(t, HEAD_DIM // 2, 1) * sin

    c0 = 3 * ATTN_WIDTH
    slot = lax.broadcasted_iota(jnp.int32, (tm, HEAD_DIM), 1)
    row_blk = (i * tm + lax.broadcasted_iota(jnp.int32, (tm, HEAD_DIM), 0)) // MOBA_BLOCK
    one_hot = (slot == row_blk).astype(BF16)
    q_blk = (i * tm + lax.broadcasted_iota(jnp.int32, (1, tm), 1)) // MOBA_BLOCK
    blk_id = lax.broadcasted_iota(jnp.int32, (n_blk, tm), 0)
    past = blk_id < q_blk

    def project(col, width=2 * HEAD_DIM):
        return jnp.dot(h, w_ref[:, col:col + width], preferred_element_type=F32)

    def key_head(pk_h, hh):
        kr = rope(_rms(pk_h, kg_ref[...]))
        kb = kr.astype(BF16)
        k_ref[hh, :, :HEAD_DIM] = kb
        k_ref[hh, :, HEAD_DIM:] = one_hot
        kn2 = jnp.max(jnp.sum(jnp.square(kb.astype(F32)), axis=1, keepdims=True), axis=0, keepdims=True)
        kmax_ref[hh:hh + 1, :] = jnp.maximum(kmax_ref[hh:hh + 1, :], kn2)
        for b in range(blocks_per_tile):
            blk = kr[b * MOBA_BLOCK:(b + 1) * MOBA_BLOCK]
            kmean_ref[hh, pl.ds(i * blocks_per_tile + b, 1), :] = jnp.mean(blk, axis=0, keepdims=True)

    def memory_head(pm_h, hh):
        sl = slice(hh * HEAD_DIM, (hh + 1) * HEAD_DIM)
        mq = (_rms(pm_h, mqg_ref[...]) * HEAD_DIM ** -0.5).astype(BF16)
        sc = lax.dot_general(mq, mk_ref[:, sl], (((1,), (1,)), ((), ())), preferred_element_type=F32)
        p = jnp.exp(sc - jnp.max(sc, axis=-1, keepdims=True))
        o = jnp.dot(p.astype(BF16), mv_ref[:, sl], preferred_element_type=F32)
        o = o / jnp.sum(p, axis=-1, keepdims=True)
        ycm_ref[:, CONV_CH + hh * HEAD_DIM:CONV_CH + (hh + 1) * HEAD_DIM] = o.astype(BF16)

    def query_head(pq_h, hh):
        qT = rope(_rms(pq_h, qg_ref[...])).T
        km_hi, km_lo = _split_bf16(kmean_ref[hh])
        q_hi, q_lo = _split_bf16(qT)
        gate = jnp.dot(jnp.concatenate([km_hi, km_hi, km_lo], axis=1),
                       jnp.concatenate([q_hi, q_lo, q_hi], axis=0),
                       preferred_element_type=F32)
        gate = jnp.where(past, gate, -jnp.inf)
        bias = jnp.where(blk_id == q_blk, 0.0, MASK_BIAS)
        for _ in range(MOBA_TOPK):
            best = jnp.max(gate, axis=0, keepdims=True)
            first = jnp.min(jnp.where(gate == best, blk_id, n_blk), axis=0, keepdims=True)
            first = jnp.where(best > -jnp.inf, first, -1)
            pick = blk_id == first
            bias = jnp.where(pick, 0.0, bias)
            gate = jnp.where(pick, -jnp.inf, gate)
        qs = (qT * (HEAD_DIM ** -0.5 * LOG2E)).astype(BF16)
        qT_ref[hh, :HEAD_DIM, :] = qs
        qn2 = jnp.sum(jnp.square(qs.astype(F32)), axis=0, keepdims=True)
        bound = jnp.sqrt(qn2 * kmax_ref[hh:hh + 1, 0:1]) * BOUND_SLACK
        bound_ref[hh] = jnp.broadcast_to(bound, (SUBLANES, tm))
        qT_ref[hh, HEAD_DIM:HEAD_DIM + n_blk, :] = bias.astype(BF16)
        qT_ref[hh, HEAD_DIM + n_blk:, :] = jnp.zeros((AUG_DIM - HEAD_DIM - n_blk, tm), BF16)

    pk = project(ATTN_WIDTH, ATTN_WIDTH)
    pq = project(0, ATTN_WIDTH)
    for hh in range(N_ATTN_HEADS):
        key_head(pk[:, hh * HEAD_DIM:(hh + 1) * HEAD_DIM], hh)
    rest = []
    for hh in range(N_ATTN_HEADS):
        if hh % 2 == 0:
            pv2 = project(2 * ATTN_WIDTH + hh * HEAD_DIM)
            for d in range(2):
                vT_ref[hh + d, 0] = pv2[:, d * HEAD_DIM:(d + 1) * HEAD_DIM].T.astype(BF16)
        rest.append(project(c0 + hh * 2 * HEAD_DIM))
        query_head(pq[:, hh * HEAD_DIM:(hh + 1) * HEAD_DIM], hh)

    rest = jnp.concatenate(rest, axis=1)
    pc = rest[:, :3 * CONV_CH]
    pm = rest[:, 3 * CONV_CH:]

    u = pc[:, CONV_CH:2 * CONV_CH] * pc[:, 2 * CONV_CH:]
    prev1 = carry_ref[SUBLANES - 1:SUBLANES, :]
    prev2 = carry_ref[SUBLANES - 2:SUBLANES - 1, :]
    row = lax.broadcasted_iota(jnp.int32, u.shape, 0)
    u1 = jnp.where(row == 0, prev1, pltpu.roll(u, 1, 0))
    u2 = jnp.where(row == 0, prev2, jnp.where(row == 1, prev1, pltpu.roll(u, 2, 0)))
    carry_ref[...] = u[tm - SUBLANES:, :]
    conv = u2 * cw_ref[0:1, :] + u1 * cw_ref[1:2, :] + u * cw_ref[2:3, :]
    ycm_ref[:, :CONV_CH] = (pc[:, :CONV_CH] * conv).astype(BF16)

    for hh in range(N_MEM_HEADS):
        memory_head(pm[:, hh * HEAD_DIM:(hh + 1) * HEAD_DIM], hh)


def _mix(x, gain, w_in, q_gain, k_gain, conv_w, cos, sin, mq_gain, mk, mv, layer, *, tm):
    s = x.shape[0]
    n_blk = s // MOBA_BLOCK
    m = mk.shape[0]
    const = lambda i: (0, 0)
    return pl.pallas_call(
        functools.partial(_mix_kernel, tm=tm),
        grid=(s // tm,),
        in_specs=[pl.BlockSpec((tm, D_MODEL), lambda i: (i, 0)),
                  _resident((1, D_MODEL), const),
                  _resident((None, D_MODEL, IN_WIDTH), lambda i: (layer, 0, 0)),
                  _resident((1, HEAD_DIM), const),
                  _resident((1, HEAD_DIM), const),
                  _resident((CONV_K, CONV_CH), const),
                  pl.BlockSpec((tm, HEAD_DIM), lambda i: (i, 0)),
                  pl.BlockSpec((tm, HEAD_DIM), lambda i: (i, 0)),
                  _resident((1, HEAD_DIM), const),
                  _resident((m, MEM_WIDTH), const),
                  _resident((m, MEM_WIDTH), const)],
        out_specs=[pl.BlockSpec((N_ATTN_HEADS, AUG_DIM, tm), lambda i: (0, 0, i)),
                   pl.BlockSpec((N_ATTN_HEADS, tm, AUG_DIM), lambda i: (0, i, 0)),
                   pl.BlockSpec((N_ATTN_HEADS, 1, HEAD_DIM, tm), lambda i: (0, i, 0, 0)),
                   pl.BlockSpec((tm, CONV_CH + MEM_WIDTH), lambda i: (i, 0)),
                   pl.BlockSpec((N_ATTN_HEADS, SUBLANES, tm), lambda i: (0, 0, i))],
        out_shape=[jax.ShapeDtypeStruct((N_ATTN_HEADS, AUG_DIM, s), BF16),
                   jax.ShapeDtypeStruct((N_ATTN_HEADS, s, AUG_DIM), BF16),
                   jax.ShapeDtypeStruct((N_ATTN_HEADS, s // tm, HEAD_DIM, tm), BF16),
                   jax.ShapeDtypeStruct((s, CONV_CH + MEM_WIDTH), BF16),
                   jax.ShapeDtypeStruct((N_ATTN_HEADS, SUBLANES, s), F32)],
        scratch_shapes=[pltpu.VMEM((N_ATTN_HEADS, n_blk, HEAD_DIM), F32),
                        pltpu.VMEM((SUBLANES, CONV_CH), F32),
                        pltpu.VMEM((N_ATTN_HEADS, HEAD_DIM), F32)],
        compiler_params=_cparams("arbitrary"),
        name="mix_proj",
    )(x, gain, w_in, q_gain, k_gain, conv_w, cos, sin, mq_gain, mk, mv)


Q_TILE = 2048
KEY_TILE = 512
M_INIT = 2 * MASK_BIAS
SAFE_LOGIT_BOUND = 40.0
BOUND_SLACK = 1.0 + 2.0 ** -7


def _causal_fix(s, first_lane_block):
    rows = []
    for b in range(s.shape[0] // MOBA_BLOCK):
        sb = s[b * MOBA_BLOCK:(b + 1) * MOBA_BLOCK]
        lo = (first_lane_block + b) * MOBA_BLOCK
        hi = lo + MOBA_BLOCK
        own = sb[:, lo:hi]
        key_pos = lax.broadcasted_iota(jnp.int32, own.shape, 0)
        qry_pos = lax.broadcasted_iota(jnp.int32, own.shape, 1)
        own = jnp.where(key_pos <= qry_pos, own, MASK_BIAS)
        pieces = ([sb[:, :lo]] if lo else []) + [own] + ([sb[:, hi:]] if hi < s.shape[1] else [])
        rows.append(jnp.concatenate(pieces, axis=1))
    return jnp.concatenate(rows, axis=0)


def _attn_kernel(qT_ref, k_ref, vT_ref, bound_ref, o_ref, acc_ref, l_ref, *, q_tile, key_tile):
    t = pl.program_id(1)
    qT = qT_ref[0]
    acc_ref[...] = jnp.zeros_like(acc_ref)
    l_ref[...] = jnp.zeros_like(l_ref)
    steps_per_tile = q_tile // key_tile
    bound = bound_ref[0, 0:1, :]
    bounded = jnp.max(bound) < SAFE_LOGIT_BOUND

    def scores(g, lane0, diagonal):
        rows = pl.ds(pl.multiple_of(g * key_tile, key_tile), key_tile)
        s = jnp.dot(k_ref[0, rows, :], qT[:, lane0:], preferred_element_type=F32)
        return _causal_fix(s, 0) if diagonal else s

    def update(g, s, m, lane0):
        m_old = m[:, lane0:]
        m_new = jnp.maximum(m_old, jnp.max(s, axis=0, keepdims=True))
        p = jnp.exp2(s - m_new)
        alpha = jnp.exp2(m_old - m_new)
        l_ref[:, lane0:] = l_ref[:, lane0:] * alpha + jnp.sum(p, axis=0, keepdims=True)
        acc_ref[:, lane0:] = (acc_ref[:, lane0:] * alpha
                              + jnp.dot(vT_ref[0, g], p.astype(BF16), preferred_element_type=F32))
        return jnp.concatenate([m[:, :lane0], m_new], axis=1) if lane0 else m_new

    def key_tiles(first, m, diagonal):
        tiles = [(first + d, d * key_tile if diagonal else 0) for d in range(steps_per_tile)]
        ss = [scores(g, lane0, diagonal) for g, lane0 in tiles]
        for (g, lane0), s in zip(tiles, ss):
            m = update(g, s, m, lane0)
        return m

    def bounded_tile(g, lane0, diagonal):
        p = jnp.exp2(scores(g, lane0, diagonal) - bound[:, lane0:])
        l_ref[:, lane0:] += jnp.sum(p, axis=0, keepdims=True)
        acc_ref[:, lane0:] += jnp.dot(vT_ref[0, g], p.astype(BF16), preferred_element_type=F32)

    def bounded_past(u, carry):
        for d in range(steps_per_tile):
            bounded_tile(u * steps_per_tile + d, 0, False)
        return carry

    @pl.when(bounded)
    def _():
        lax.fori_loop(0, t, bounded_past, 0)
        for d in range(steps_per_tile):
            bounded_tile(t * steps_per_tile + d, d * key_tile, True)

    @pl.when(jnp.logical_not(bounded))
    def _():
        m = jnp.full((1, q_tile), M_INIT, F32)
        m = lax.fori_loop(0, t, lambda u, m: key_tiles(u * steps_per_tile, m, False), m)
        key_tiles(t * steps_per_tile, m, True)

    out = acc_ref[...] / l_ref[...]
    o_ref[...] = out.T.astype(o_ref.dtype)


def _attention(qT, k, vT, bound):
    s = k.shape[1]
    key_tile = vT.shape[3]
    q_tile = min(Q_TILE, s)
    return pl.pallas_call(
        functools.partial(_attn_kernel, q_tile=q_tile, key_tile=key_tile),
        grid=(N_ATTN_HEADS, s // q_tile),
        in_specs=[pl.BlockSpec((1, AUG_DIM, q_tile), lambda h, t: (h, 0, t)),
                  pl.BlockSpec((1, s, AUG_DIM), lambda h, t: (h, 0, 0)),
                  pl.BlockSpec((1, s // key_tile, HEAD_DIM, key_tile), lambda h, t: (h, 0, 0, 0)),
                  pl.BlockSpec((1, SUBLANES, q_tile), lambda h, t: (h, 0, t))],
        out_specs=pl.BlockSpec((q_tile, HEAD_DIM), lambda h, t: (t, h)),
        out_shape=jax.ShapeDtypeStruct((s, ATTN_WIDTH), BF16),
        scratch_shapes=[pltpu.VMEM((HEAD_DIM, q_tile), F32), pltpu.VMEM((1, q_tile), F32)],
        compiler_params=_cparams("parallel", "arbitrary"),
        name="moba_attn",
    )(qT, k, vT, bound)


def _outproj_kernel(x_ref, ya_ref, ycm_ref, w_ref, o_ref):
    o_ref[...] = (x_ref[...]
                  + jnp.dot(ya_ref[...], w_ref[:ATTN_WIDTH], preferred_element_type=F32)
                  + jnp.dot(ycm_ref[...], w_ref[ATTN_WIDTH:], preferred_element_type=F32))


def _outproj(x, y_attn, y_cm, w_out, layer, *, tm):
    s = x.shape[0]
    return pl.pallas_call(
        _outproj_kernel,
        grid=(s // tm,),
        in_specs=[pl.BlockSpec((tm, D_MODEL), lambda i: (i, 0)),
                  pl.BlockSpec((tm, ATTN_WIDTH), lambda i: (i, 0)),
                  pl.BlockSpec((tm, CONV_CH + MEM_WIDTH), lambda i: (i, 0)),
                  _resident((None, D_MODEL, D_MODEL), lambda i: (layer, 0, 0))],
        out_specs=pl.BlockSpec((tm, D_MODEL), lambda i: (i, 0)),
        out_shape=jax.ShapeDtypeStruct((s, D_MODEL), F32),
        compiler_params=_cparams("parallel"),
        name="out_proj",
    )(x, y_attn, y_cm, w_out)


def kernel(x, mem, positions, ffn1_norm, ffn1_w_gate_up, ffn1_w_down, mix_norm, w_in, q_norm, k_norm,
           conv_w, mem_norm, w_mem_kv, mq_norm, mk_norm, w_out, ffn2_norm, ffn2_w_gate_up, ffn2_w_down):
    b, s, _ = x.shape
    assert b == 1 and s % KEY_TILE == 0 and s // MOBA_BLOCK <= AUG_DIM - HEAD_DIM
    depth = w_in.shape[0]
    tm = min(s, KEY_TILE)
    ffn = functools.partial(_ffn, tm=min(s, ROW_TILE), tf=FFN_FF_TILE)

    ffn_weights = [(ffn1_w_gate_up, ffn1_w_down), (ffn2_w_gate_up, ffn2_w_down)]
    w_gu, w_d = (w[0].astype(BF16) for w in ffn_weights[0])
    ffn_gains = [ffn1_norm, ffn2_norm]
    w_in_b, w_out_b, w_mkv = (w.astype(BF16) for w in (w_in, w_out, w_mem_kv))
    cos, sin = _rope_tables(positions[0])
    xs = x[0]
    mem2 = mem[0]

    def ffn_call(xs, which, l, w_gu, w_d):
        nxt = 2 * l + which + 1
        next_weights = None if nxt == 2 * depth else ffn_weights[nxt % 2] + (nxt // 2,)
        out = ffn(xs, ffn_gains[which][l][None], w_gu, w_d, next_weights)
        return tuple(out) if next_weights else (out[0], None, None)

    for l in range(depth):
        xs, w_gu, w_d = ffn_call(xs, 0, l, w_gu, w_d)
        mk, mv = _memkv(mem2, mem_norm[l][None], w_mkv, mk_norm[l][None], l)
        qT, k, vT, y_cm, bound = _mix(xs, mix_norm[l][None], w_in_b, q_norm[l][None],
                                      k_norm[l][None], conv_w[l], cos, sin, mq_norm[l][None],
                                      mk, mv, l, tm=tm)
        y_attn = _attention(qT, k, vT, bound)
        xs = _outproj(xs, y_attn, y_cm, w_out_b, l, tm=min(s, ROW_TILE))
        xs, w_gu, w_d = ffn_call(xs, 1, l, w_gu, w_d)
    return xs[None]
```
